```python
import jax, jax.numpy as jnp
from jax import lax
import numpy as np

D_MODEL = 1024
BATCH = 8
SEQ = 8192
DEPTH = 2
DEC_BATCH = 4
DEC_SEQ = 4096
PAST_LEN = 128

GRID_W = 64
PLE_DIM = 256
GLA_HEADS = 4
GLA_DK = 64
GLA_DV = 128
GLA_RANK = 16
GLA_NORMALIZER = 16.0
GLA_CHUNK = 64
NA_HEADS = 8
NA_DH = 64
NA_KH = 8
NA_KW = 16
GLA_QK = GLA_HEADS * GLA_DK
GLA_V = GLA_HEADS * GLA_DV
NA_W = NA_HEADS * NA_DH
MIX_W = GLA_V + NA_W
IN_W = 2 * GLA_QK + 2 * GLA_V + 2 * GLA_RANK + 3 * NA_W
N_EXPERTS = 32
TOP_K = 4
D_EXPERT = D_MODEL
SWIGLU_ALPHA = 1.702
SWIGLU_LIMIT = 7.0
MOE_BLOCK = 256
DEEP_ALPHA = (2 * DEPTH) ** 0.25
DEEP_BETA = (8 * DEPTH) ** -0.25
LN_EPS = 1e-5
RMS_EPS = 1e-6

kernel_name = 'hybrid_gla_natten_moe_encoder'


def _layer_norm(x, g, b):
    xf = x.astype(jnp.float32)
    mu = jnp.mean(xf, axis=-1, keepdims=True)
    var = jnp.mean(jnp.square(xf - mu), axis=-1, keepdims=True)
    return ((xf - mu) * lax.rsqrt(var + LN_EPS) * g + b).astype(x.dtype)


def _heads(a, h):
    B, T, _ = a.shape
    return a.reshape(B, T, h, -1).transpose(0, 2, 1, 3)


def _gla_direction(q, k, v, logg, strict):
    B, H, T, dk = q.shape
    dv = v.shape[-1]
    C = GLA_CHUNK
    n = T // C

    def chunks(a):
        return a.reshape(B, H, n, C, a.shape[-1]).transpose(2, 0, 1, 3, 4)

    idx = jnp.arange(C)
    mask = (idx[:, None] > idx[None, :]) if strict else (idx[:, None] >= idx[None, :])

    def step(S, inp):
        qi, ki, vi, gi = inp
        b = jnp.cumsum(gi, axis=2)
        diff = b[:, :, :, None, :] - b[:, :, None, :, :]
        decay = jnp.exp(jnp.where(mask[:, :, None], diff, -jnp.inf))
        A = jnp.einsum('bhid,bhjd,bhijd->bhij', qi, ki, decay)
        o = jnp.einsum('bhij,bhjv->bhiv', A, vi) + jnp.einsum('bhid,bhdv->bhiv', qi * jnp.exp(b), S)
        b_last = b[:, :, -1:, :]
        S = jnp.exp(b_last[:, :, 0, :])[..., None] * S + jnp.einsum('bhjd,bhjv->bhdv', ki * jnp.exp(b_last - b), vi)
        return S, o

    S0 = jnp.zeros((B, H, dk, dv), jnp.float32)
    _, o = lax.scan(step, S0, (chunks(q), chunks(k), chunks(v), chunks(logg)))
    return o.transpose(1, 2, 0, 3, 4).reshape(B, H, T, dv)


def _gla_group(q, k, v, g_out, lr_f, lr_b, w_gk_f, b_gk_f, w_gk_b, b_gk_b, norm_g):
    B, T, _ = q.shape
    f32 = jnp.float32
    qh = _heads(q, GLA_HEADS).astype(f32) * GLA_DK ** -0.5
    kh = _heads(k, GLA_HEADS).astype(f32)
    vh = _heads(v, GLA_HEADS).astype(f32)
    lg_f = _heads(jax.nn.log_sigmoid((lr_f @ w_gk_f + b_gk_f).astype(f32)) / GLA_NORMALIZER, GLA_HEADS)
    lg_b = _heads(jax.nn.log_sigmoid((lr_b @ w_gk_b + b_gk_b).astype(f32)) / GLA_NORMALIZER, GLA_HEADS)
    flip = lambda a: jnp.flip(a, axis=2)
    o_f = _gla_direction(qh, kh, vh, lg_f, strict=False)
    o_b = flip(_gla_direction(flip(qh), flip(kh), flip(vh), flip(lg_b), strict=True))
    o = (o_f + o_b).transpose(0, 2, 1, 3)
    o = o * lax.rsqrt(jnp.mean(jnp.square(o), axis=-1, keepdims=True) + RMS_EPS)
    o = o * norm_g.astype(f32).reshape(GLA_HEADS, GLA_DV)
    o = o.reshape(B, T, GLA_V) * jax.nn.silu(g_out.astype(f32))
    return o.astype(q.dtype)


def _na_group(q, k, v, rpb):
    B, T, _ = q.shape
    rows = T // GRID_W
    kh = min(NA_KH, rows)

    def grid(a):
        return a.reshape(B, rows, GRID_W, NA_HEADS, NA_DH).transpose(1, 0, 3, 2, 4)

    qg = grid(q) * NA_DH ** -0.5
    kg = grid(k)
    vg = grid(v)
    col = jnp.arange(GRID_W)
    col_idx = jnp.clip(col - NA_KW // 2, 0, GRID_W - NA_KW)[:, None] + jnp.arange(NA_KW)[None, :]
    col_off = col_idx - col[:, None] + (NA_KW - 1)
    row = jnp.arange(rows)
    row_start = jnp.clip(row - kh // 2, 0, rows - kh)

    def row_block(inp):
        qr, r, rs = inp
        kr = lax.dynamic_slice_in_dim(kg, rs, kh, axis=0)[:, :, :, col_idx, :]
        vr = lax.dynamic_slice_in_dim(vg, rs, kh, axis=0)[:, :, :, col_idx, :]
        s = jnp.einsum('bhcd,abhcwd->bhcaw', qr, kr).astype(jnp.float32)
        row_off = rs + jnp.arange(kh) - r + (NA_KH - 1)
        bias = rpb[:, row_off][:, :, col_off].transpose(0, 2, 1, 3)
        s = s + bias.astype(jnp.float32)[None]
        p = jax.nn.softmax(s.reshape(B, NA_HEADS, GRID_W, kh * NA_KW), axis=-1)
        p = p.reshape(B, NA_HEADS, GRID_W, kh, NA_KW).astype(vr.dtype)
        return jnp.einsum('bhcaw,abhcwd->bhcd', p, vr)

    o = lax.map(row_block, (qg, row, row_start))
    return o.transpose(1, 0, 3, 2, 4).reshape(B, T, NA_W)


def _moe(x, w_router, b_router, w_gu, b_gu, w_down, b_down):
    B, T, D = x.shape
    N = B * T
    f32 = jnp.float32
    xf = x.reshape(N, D)
    logits = xf.astype(f32) @ w_router.astype(f32) + b_router.astype(f32)
    top_v, top_e = lax.top_k(logits, TOP_K)
    probs = jax.nn.softmax(top_v, axis=-1).astype(x.dtype)
    NK = N * TOP_K
    C = MOE_BLOCK
    flat_e = top_e.reshape(NK)
    flat_t = jnp.arange(NK, dtype=jnp.int32) // TOP_K
    flat_p = probs.reshape(NK)
    order = jnp.argsort(flat_e)
    se = flat_e[order]
    counts = jnp.bincount(flat_e, length=N_EXPERTS)
    padded = (counts + C - 1) // C * C
    pad_end = jnp.cumsum(padded)
    pad_start = pad_end - padded
    start = jnp.cumsum(counts) - counts
    dest = pad_start[se] + jnp.arange(NK, dtype=jnp.int32) - start[se]
    n_blocks = (NK + N_EXPERTS * (C - 1) + C - 1) // C
    L = n_blocks * C
    tok = jnp.full((L,), N, jnp.int32).at[dest].set(flat_t[order])
    wts = jnp.zeros((L,), x.dtype).at[dest].set(flat_p[order])
    blk_e = jnp.minimum(jnp.searchsorted(pad_end, jnp.arange(n_blocks) * C, side='right'), N_EXPERTS - 1)
    x_pad = jnp.concatenate([xf, jnp.zeros((1, D), x.dtype)], axis=0)

    def step(acc, inp):
        t, w, e = inp
        h = x_pad[t] @ w_gu[e] + b_gu[e]
        gate = jnp.minimum(h[:, :D_EXPERT], SWIGLU_LIMIT)
        up = jnp.clip(h[:, D_EXPERT:], -SWIGLU_LIMIT, SWIGLU_LIMIT)
        glu = gate * jax.nn.sigmoid(gate * SWIGLU_ALPHA)
        y = ((up + 1.0) * glu) @ w_down[e] + b_down[e]
        return acc.at[t].add(y * w[:, None]), None

    acc0 = jnp.zeros((N + 1, D), x.dtype)
    acc, _ = lax.scan(step, acc0, (tok.reshape(n_blocks, C), wts.reshape(n_blocks, C), blk_e))
    return acc[:N].reshape(B, T, D)


def _layer(x, p, w_in, w_gk_f, b_gk_f, w_gk_b, b_gk_b, gla_norm_g, rpb, w_out, ln1_g, ln1_b,
           w_router, b_router, w_gu, b_gu, w_down, b_down, w_ple_proj, w_ple_gate, b_ple_gate, ln2_g, ln2_b):
    z = x @ w_in
    sizes = (GLA_QK, GLA_QK, GLA_V, GLA_V, GLA_RANK, GLA_RANK, NA_W, NA_W)
    cuts = [int(c) for c in np.cumsum(sizes)]
    q_g, k_g, v_g, g_g, lr_f, lr_b, q_n, k_n, v_n = jnp.split(z, cuts, axis=-1)
    o_gla = _gla_group(q_g, k_g, v_g, g_g, lr_f, lr_b, w_gk_f, b_gk_f, w_gk_b, b_gk_b, gla_norm_g)
    o_na = _na_group(q_n, k_n, v_n, rpb)
    mix = jnp.concatenate([o_gla, o_na], axis=-1) @ w_out
    x = _layer_norm(DEEP_ALPHA * x + mix, ln1_g, ln1_b)
    r = DEEP_ALPHA * x + _moe(x, w_router, b_router, w_gu, b_gu, w_down, b_down)
    u = (p @ w_ple_proj) * jax.nn.sigmoid(r @ w_ple_gate + b_ple_gate)
    return _layer_norm(r + u, ln2_g, ln2_b)


def _trunk(x, p, emb_ln_g, emb_ln_b, w_in, w_gk_f, b_gk_f, w_gk_b, b_gk_b, gla_norm_g, rpb, w_out,
           ln1_g, ln1_b, w_router, b_router, w_gu, b_gu, w_down, b_down, w_ple_proj, w_ple_gate,
           b_ple_gate, ln2_g, ln2_b):
    x = _layer_norm(x, emb_ln_g, emb_ln_b)
    for i in range(DEPTH):
        x = _layer(x, p[i], w_in[i], w_gk_f[i], b_gk_f[i], w_gk_b[i], b_gk_b[i], gla_norm_g[i], rpb[i],
                   w_out[i], ln1_g[i], ln1_b[i], w_router[i], b_router[i], w_gu[i], b_gu[i], w_down[i],
                   b_down[i], w_ple_proj[i], w_ple_gate[i], b_ple_gate[i], ln2_g[i], ln2_b[i])
    return x


def setup_inputs(seed: int = 0) -> dict:
    key = jax.random.key(seed)
    ks = jax.random.split(key, 28)
    f32 = jnp.float32

    def nrm(k, shape, scale):
        return jax.random.normal(k, shape, f32) * scale

    L = DEPTH
    E = N_EXPERTS
    return {
        'x_prompt': nrm(ks[0], (BATCH, SEQ, D_MODEL), 1.0),
        'x_sample': nrm(ks[1], (DEC_BATCH, DEC_SEQ, D_MODEL), 1.0),
        'p_prompt': nrm(ks[2], (DEPTH, BATCH, SEQ, PLE_DIM), 1.0),
        'p_sample': nrm(ks[3], (DEPTH, DEC_BATCH, DEC_SEQ, PLE_DIM), 1.0),
        'emb_ln_g': 1.0 + nrm(ks[4], (D_MODEL,), 0.02),
        'emb_ln_b': nrm(ks[5], (D_MODEL,), 0.02),
        'w_in': nrm(ks[6], (L, D_MODEL, IN_W), D_MODEL ** -0.5),
        'w_gk_f': nrm(ks[7], (L, GLA_RANK, GLA_QK), GLA_RANK ** -0.5),
        'b_gk_f': nrm(ks[8], (L, GLA_QK), 0.5),
        'w_gk_b': nrm(ks[9], (L, GLA_RANK, GLA_QK), GLA_RANK ** -0.5),
        'b_gk_b': nrm(ks[10], (L, GLA_QK), 0.5),
        'gla_norm_g': 1.0 + nrm(ks[11], (L, GLA_V), 0.02),
        'rpb': nrm(ks[12], (L, NA_HEADS, 2 * NA_KH - 1, 2 * NA_KW - 1), 0.1),
        'w_out': nrm(ks[13], (L, MIX_W, D_MODEL), MIX_W ** -0.5 * DEEP_BETA),
        'ln1_g': 1.0 + nrm(ks[14], (L, D_MODEL), 0.02),
        'ln1_b': nrm(ks[15], (L, D_MODEL), 0.02),
        'w_router': nrm(ks[16], (L, D_MODEL, E), D_MODEL ** -0.5),
        'b_router': nrm(ks[17], (L, E), 0.01),
        'w_gu': nrm(ks[18], (L, E, D_MODEL, 2 * D_EXPERT), D_MODEL ** -0.5),
        'b_gu': nrm(ks[19], (L, E, 2 * D_EXPERT), 0.02),
        'w_down': nrm(ks[20], (L, E, D_EXPERT, D_MODEL), D_EXPERT ** -0.5 * DEEP_BETA),
        'b_down': nrm(ks[21], (L, E, D_MODEL), 0.02),
        'w_ple_proj': nrm(ks[22], (L, PLE_DIM, D_MODEL), PLE_DIM ** -0.5 * DEEP_BETA),
        'w_ple_gate': nrm(ks[23], (L, D_MODEL, D_MODEL), D_MODEL ** -0.5),
        'b_ple_gate': nrm(ks[24], (L, D_MODEL), 0.02),
        'ln2_g': 1.0 + nrm(ks[25], (L, D_MODEL), 0.02),
        'ln2_b': nrm(ks[26], (L, D_MODEL), 0.02),
    }


def reference(x_prompt, x_sample, p_prompt, p_sample, emb_ln_g, emb_ln_b, w_in, w_gk_f, b_gk_f, w_gk_b,
              b_gk_b, gla_norm_g, rpb, w_out, ln1_g, ln1_b, w_router, b_router, w_gu, b_gu, w_down, b_down,
              w_ple_proj, w_ple_gate, b_ple_gate, ln2_g, ln2_b):
    y_prompt = _trunk(x_prompt, p_prompt, emb_ln_g, emb_ln_b, w_in, w_gk_f, b_gk_f, w_gk_b, b_gk_b,
                      gla_norm_g, rpb, w_out, ln1_g, ln1_b, w_router, b_router, w_gu, b_gu, w_down, b_down,
                      w_ple_proj, w_ple_gate, b_ple_gate, ln2_g, ln2_b)
    y_sample = _trunk(x_sample, p_sample, emb_ln_g, emb_ln_b, w_in, w_gk_f, b_gk_f, w_gk_b, b_gk_b,
                      gla_norm_g, rpb, w_out, ln1_g, ln1_b, w_router, b_router, w_gu, b_gu, w_down, b_down,
                      w_ple_proj, w_ple_gate, b_ple_gate, ln2_g, ln2_b)
    return (y_prompt, y_sample)
```

```python
import functools

import numpy as np
import jax
import jax.numpy as jnp
from jax import lax
from jax.experimental import pallas as pl
from jax.experimental.pallas import tpu as pltpu

GRID_W = 64
GLA_HEADS, GLA_DK, GLA_DV, GLA_RANK = 4, 64, 128, 16
GLA_NORMALIZER = 16.0
NA_HEADS, NA_DH, NA_KH, NA_KW = 8, 64, 8, 16
N_EXPERTS, TOP_K = 32, 4
SWIGLU_ALPHA, SWIGLU_LIMIT = 1.702, 7.0
LN_EPS, RMS_EPS = 1e-5, 1e-6

GLA_QK = GLA_HEADS * GLA_DK
GLA_V = GLA_HEADS * GLA_DV
NA_W = NA_HEADS * NA_DH

LANES = 128
V7X_VMEM_LIMIT_BYTES = 56 * 1024 * 1024

TOKEN_BLOCK = 512
GLA_CHUNK = 128
GLA_SAFE_LOG_DECAY = 40.0
NA_ROWS_PER_BLOCK = TOKEN_BLOCK // GRID_W
MOE_ROWS = 512
FINAL_BLOCK = 256
NEG_BIG = -1e30

F32 = jnp.float32
BF16 = jnp.bfloat16


def _cparams(*sem):
    return pltpu.CompilerParams(dimension_semantics=sem, vmem_limit_bytes=V7X_VMEM_LIMIT_BYTES)


def _dot(a, b, precision=None):
    return jnp.dot(a, b, preferred_element_type=F32, precision=precision)


def _dot_nt(a, b, precision=None):
    return lax.dot_general(a, b, (((1,), (1,)), ((), ())), preferred_element_type=F32, precision=precision)


def _dot_tn(a, b):
    return lax.dot_general(a, b, (((0,), (0,)), ((), ())), preferred_element_type=F32)


def _layer_norm(x, g, b):
    mu = jnp.mean(x, axis=-1, keepdims=True)
    xc = x - mu
    var = jnp.mean(xc * xc, axis=-1, keepdims=True)
    return xc * lax.rsqrt(var + LN_EPS) * g + b


def _slab_load(ref, n, width):
    per = width // LANES
    return jnp.concatenate([ref[pl.ds(s, n, stride=per), :] for s in range(per)], axis=-1)


def _slab_store(ref, x):
    n, width = x.shape
    per = width // LANES
    for s in range(per):
        ref[pl.ds(s, n, stride=per), :] = x[:, s * LANES:(s + 1) * LANES]


def _seq_local(blk, groups):
    loc = lax.rem(blk - groups[-1][0], groups[-1][1])
    bps = jnp.int32(groups[-1][1])
    for (first, per), nxt in zip(reversed(groups[:-1]), reversed(groups[1:])):
        inside = blk < nxt[0]
        loc = jnp.where(inside, lax.rem(blk - first, per), loc)
        bps = jnp.where(inside, per, bps)
    return loc, bps


def _inproj_kernel(x_ref, g_ref, b_ref, wg_ref, wlr_ref, wgk_ref, bgk_ref, wn_ref, *out_refs, apply_ln):
    if apply_ln:
        x0_ref, qk_ref, v_ref, gg_ref, lg_ref, qn_ref, kn_ref, vn_ref = out_refs
    else:
        qk_ref, v_ref, gg_ref, lg_ref, qn_ref, kn_ref, vn_ref = out_refs
    x = x_ref[...]
    if apply_ln:
        x = _layer_norm(x, g_ref[...], b_ref[...])
        x0_ref[...] = x
    xb = x.astype(BF16)
    zg = _dot(xb, wg_ref[...])
    qk_ref[:, :GLA_QK] = (zg[:, :GLA_QK] * GLA_DK ** -0.5).astype(BF16)
    qk_ref[:, GLA_QK:] = zg[:, GLA_QK:2 * GLA_QK].astype(BF16)
    v_ref[...] = zg[:, 2 * GLA_QK:2 * GLA_QK + GLA_V].astype(BF16)
    gg_ref[...] = zg[:, 2 * GLA_QK + GLA_V:]
    lr = _dot(xb, wlr_ref[...])
    pre = _dot(lr.astype(BF16), wgk_ref[...]) + bgk_ref[...]
    log_sig = jnp.minimum(pre, 0.0) - jnp.log1p(jnp.exp(-jnp.abs(pre)))
    lg_ref[...] = log_sig * (1.0 / GLA_NORMALIZER)
    zn = _dot(xb, wn_ref[...])
    qn_ref[...] = (zn[:, :NA_W] * NA_DH ** -0.5).astype(BF16)
    kn_ref[...] = zn[:, NA_W:2 * NA_W].astype(BF16)
    vn_ref[...] = zn[:, 2 * NA_W:].astype(BF16)


def _inproj(x, ln_g, ln_b, wg, wlr, wgk, bgk, wn, *, apply_ln):
    n, d = x.shape
    tm = TOKEN_BLOCK
    row = lambda w: pl.BlockSpec((tm, w), lambda i: (i, 0))
    full = lambda a: pl.BlockSpec(a.shape, lambda i: (0,) * a.ndim)
    outs = [(GLA_QK * 2, BF16), (GLA_V, BF16), (GLA_V, F32), (2 * GLA_QK, F32),
            (NA_W, BF16), (NA_W, BF16), (NA_W, BF16)]
    if apply_ln:
        outs = [(d, F32)] + outs
    return pl.pallas_call(
        functools.partial(_inproj_kernel, apply_ln=apply_ln),
        grid=(n // tm,),
        in_specs=[row(d), full(ln_g), full(ln_b), full(wg), full(wlr), full(wgk), full(bgk), full(wn)],
        out_specs=[row(w) for w, _ in outs],
        out_shape=[jax.ShapeDtypeStruct((n, w), t) for w, t in outs],
        compiler_params=_cparams("parallel"),
        name="inproj_ln" if apply_ln else "inproj",
    )(x, ln_g, ln_b, wg, wlr, wgk, bgk, wn)


def _gla_chunk_fast(qk_ref, v_ref, lg_ref, o_ref, s_ref, d, c, fwd):
    C = GLA_CHUNK
    rows = pl.ds(c * C, C)
    r_i = lax.broadcasted_iota(jnp.int32, (C, C), 0)
    c_i = lax.broadcasted_iota(jnp.int32, (C, C), 1)
    tri = (r_i >= c_i) if fwd else (r_i <= c_i)
    lg = lg_ref[rows, :]
    cum = _dot(tri.astype(F32), lg, precision=lax.Precision.HIGHEST)
    tot = cum[C - 1:C, :] if fwd else cum[0:1, :]
    q = qk_ref[rows, :GLA_QK].astype(F32)
    k = qk_ref[rows, GLA_QK:].astype(F32)
    qt = (q * jnp.exp(cum)).astype(BF16)
    kt = (k * jnp.exp(-cum)).astype(BF16)
    kd = (k * jnp.exp(tot - cum)).astype(BF16)
    dec = jnp.exp(tot)
    mask = (r_i >= c_i) if fwd else (r_i < c_i)
    for h in range(GLA_HEADS):
        ks = slice(h * GLA_DK, (h + 1) * GLA_DK)
        vs = slice(h * GLA_DV, (h + 1) * GLA_DV)
        a = jnp.where(mask, _dot_nt(qt[:, ks], kt[:, ks]), 0.0)
        v = v_ref[rows, vs]
        s = s_ref[d, h]
        o_ref[rows, vs] = _dot(a.astype(BF16), v) + _dot_nt(qt[:, ks], s.astype(BF16))
        s_ref[d, h] = dec[:, ks] * s + _dot_tn(v, kd[:, ks])


def _gla_block_slow(qk_ref, v_ref, lg_ref, o_ref, s_ref, d, fwd, tb):
    G = 16

    def group(n, carry):
        rows = pl.ds(pl.multiple_of((n if fwd else tb // G - 1 - n) * G, G), G)
        gate = jnp.exp(lg_ref[rows, :])
        q = qk_ref[rows, :GLA_QK].astype(F32)
        k = qk_ref[rows, GLA_QK:]
        qs = (q if fwd else q * gate).astype(BF16)
        row_q = lax.broadcasted_iota(jnp.int32, (G, GLA_DK), 0)
        row_v = lax.broadcasted_iota(jnp.int32, (G, GLA_DV), 0)
        for h in range(GLA_HEADS):
            ks = slice(h * GLA_DK, (h + 1) * GLA_DK)
            vs = slice(h * GLA_DV, (h + 1) * GLA_DV)
            v = v_ref[rows, vs]
            s = s_ref[d, h]
            o = jnp.zeros((G, GLA_DV), F32)
            for r in (range(G) if fwd else reversed(range(G))):
                v_r = jnp.where(row_v == r, v, jnp.zeros_like(v))
                q_r = jnp.where(row_q == r, qs[:, ks], jnp.zeros_like(qs[:, ks]))
                s_new = gate[r:r + 1, ks] * s + _dot_tn(v_r, k[:, ks])
                o = o + _dot_nt(q_r, (s_new if fwd else s).astype(BF16))
                s = s_new
            o_ref[rows, vs] = o
            s_ref[d, h] = s
        return carry

    lax.fori_loop(0, tb // G, group, 0)


def _gla_kernel(qkf_ref, vf_ref, lgf_ref, qkb_ref, vb_ref, lgb_ref, of_ref, ob_ref, s_ref, *, groups, tb):
    i = pl.program_id(0)
    nblk = pl.num_programs(0)
    loc_f, _ = _seq_local(i, groups)
    loc_b, bps_b = _seq_local(nblk - 1 - i, groups)

    @pl.when(loc_f == 0)
    def _():
        s_ref[0] = jnp.zeros(s_ref.shape[1:], F32)

    @pl.when(loc_b == bps_b - 1)
    def _():
        s_ref[1] = jnp.zeros(s_ref.shape[1:], F32)

    nc = tb // GLA_CHUNK
    worst = jnp.float32(0.0)
    for ref in (lgf_ref, lgb_ref):
        for c in range(nc):
            tot = jnp.sum(ref[pl.ds(c * GLA_CHUNK, GLA_CHUNK), :], axis=0, keepdims=True)
            worst = jnp.minimum(worst, jnp.min(tot))
    safe = worst > -GLA_SAFE_LOG_DECAY

    @pl.when(safe)
    def _():
        for c in range(nc):
            _gla_chunk_fast(qkf_ref, vf_ref, lgf_ref, of_ref, s_ref, 0, c, True)
            _gla_chunk_fast(qkb_ref, vb_ref, lgb_ref, ob_ref, s_ref, 1, nc - 1 - c, False)

    @pl.when(jnp.logical_not(safe))
    def _():
        _gla_block_slow(qkf_ref, vf_ref, lgf_ref, of_ref, s_ref, 0, True, tb)
        _gla_block_slow(qkb_ref, vb_ref, lgb_ref, ob_ref, s_ref, 1, False, tb)


def _gla(qk, v, lg, groups):
    n = qk.shape[0]
    tb = TOKEN_BLOCK
    nblk = n // tb
    fwd = lambda w, j: pl.BlockSpec((tb, w), lambda i: (i, j))
    bwd = lambda w, j: pl.BlockSpec((tb, w), lambda i: (nblk - 1 - i, j))
    return pl.pallas_call(
        functools.partial(_gla_kernel, groups=groups, tb=tb),
        grid=(nblk,),
        in_specs=[fwd(2 * GLA_QK, 0), fwd(GLA_V, 0), fwd(GLA_QK, 0),
                  bwd(2 * GLA_QK, 0), bwd(GLA_V, 0), bwd(GLA_QK, 1)],
        out_specs=[fwd(GLA_V, 0), bwd(GLA_V, 0)],
        out_shape=[jax.ShapeDtypeStruct((n, GLA_V), F32)] * 2,
        scratch_shapes=[pltpu.VMEM((2, GLA_HEADS, GLA_DV, GLA_DK), F32)],
        compiler_params=_cparams("arbitrary"),
        name="gla",
    )(qk, v, lg, qk, v, lg)


def _na_bias_table(rpb):
    w = GRID_W
    col = np.arange(w)
    start = np.clip(col - NA_KW // 2, 0, w - NA_KW)
    kc = np.arange(w)
    in_win = (kc[None, :] >= start[:, None]) & (kc[None, :] < start[:, None] + NA_KW)
    col_off = np.clip(kc[None, :] - col[:, None] + (NA_KW - 1), 0, 2 * NA_KW - 2)
    shift = np.arange(NA_KH)
    row_off = np.arange(NA_KH)[None, :] - shift[:, None] + (NA_KH - 1)
    t = rpb[:, row_off]
    t = t[:, :, :, col_off]
    t = jnp.where(jnp.asarray(in_win)[None, None, None], t, NEG_BIG)
    t = t.transpose(1, 0, 3, 2, 4)
    return t.reshape(NA_KH, NA_HEADS, w, NA_KH * w).astype(F32)


def _na_kernel(q_ref, kp_ref, kc_ref, kn_ref, vp_ref, vc_ref, vn_ref, bias_ref, o_ref, kbuf, vbuf, *, groups):
    tb = TOKEN_BLOCK
    w = GRID_W
    rpb_rows = NA_ROWS_PER_BLOCK
    blk = pl.program_id(1)
    loc, bps = _seq_local(blk, groups)
    rows_in_seq = bps * rpb_rows
    for j, (kr, vr) in enumerate(((kp_ref, vp_ref), (kc_ref, vc_ref), (kn_ref, vn_ref))):
        kbuf[pl.ds(j * tb, tb), :] = kr[...]
        vbuf[pl.ds(j * tb, tb), :] = vr[...]
    lane = lax.broadcasted_iota(jnp.int32, (w, LANES), 1)
    win = NA_KH * w
    for j in range(rpb_rows):
        r = loc * rpb_rows + j
        rs = jnp.clip(r - NA_KH // 2, 0, rows_in_seq - NA_KH)
        off = pl.multiple_of((rs - loc * rpb_rows + rpb_rows) * w, w)
        shift = r - rs
        q = q_ref[pl.ds(j * w, w), :]
        kwin = kbuf[pl.ds(off, win), :]
        vwin = vbuf[pl.ds(off, win), :]
        out = None
        for hh in range(LANES // NA_DH):
            in_head = (lane >= hh * NA_DH) & (lane < (hh + 1) * NA_DH)
            s = _dot_nt(jnp.where(in_head, q, jnp.zeros_like(q)), kwin) + bias_ref[shift, hh]
            m = jnp.max(s, axis=-1, keepdims=True)
            p = jnp.exp(s - m)
            l = jnp.sum(p, axis=-1, keepdims=True)
            o = _dot(p.astype(BF16), vwin) / l
            out = o if out is None else jnp.where(in_head, o, out)
        o_ref[pl.ds(j * w, w), :] = out.astype(o_ref.dtype)


def _na(q, k, v, bias, groups):
    n = q.shape[0]
    tb = TOKEN_BLOCK
    nblk = n // tb
    nhp = NA_W // LANES
    cur = pl.BlockSpec((tb, LANES), lambda hp, i: (i, hp))
    prv = pl.BlockSpec((tb, LANES), lambda hp, i: (jnp.maximum(i - 1, 0), hp))
    nxt = pl.BlockSpec((tb, LANES), lambda hp, i: (jnp.minimum(i + 1, nblk - 1), hp))
    heads_per = LANES // NA_DH
    return pl.pallas_call(
        functools.partial(_na_kernel, groups=groups),
        grid=(nhp, nblk),
        in_specs=[cur, prv, cur, nxt, prv, cur, nxt,
                  pl.BlockSpec((NA_KH, heads_per, GRID_W, NA_KH * GRID_W), lambda hp, i: (0, hp, 0, 0))],
        out_specs=cur,
        out_shape=jax.ShapeDtypeStruct((n, NA_W), BF16),
        scratch_shapes=[pltpu.VMEM((3 * tb, LANES), BF16), pltpu.VMEM((3 * tb, LANES), BF16)],
        compiler_params=_cparams("parallel", "parallel"),
        name="natten",
    )(q, k, k, k, v, v, v, bias)


def _outproj_kernel(of_ref, ob_ref, gg_ref, ona_ref, x_ref, ng_ref, wog_ref, won_ref, g1_ref, b1_ref,
                    wr_ref, br_ref, x1_ref, e_ref, p_ref, pos_ref, cnt_ref, run_ref, lt_ref,
                    *, alpha):
    i = pl.program_id(0)
    tm = x_ref.shape[0]

    @pl.when(i == 0)
    def _():
        run_ref[...] = jnp.zeros(run_ref.shape, F32)
        r_i = lax.broadcasted_iota(jnp.int32, (tm, tm), 0)
        c_i = lax.broadcasted_iota(jnp.int32, (tm, tm), 1)
        lt_ref[...] = (c_i < r_i).astype(BF16)

    o = of_ref[...] + ob_ref[...]
    parts = []
    for h in range(GLA_HEADS):
        oh = o[:, h * GLA_DV:(h + 1) * GLA_DV]
        parts.append(oh * lax.rsqrt(jnp.mean(oh * oh, axis=-1, keepdims=True) + RMS_EPS))
    g = gg_ref[...]
    o = jnp.concatenate(parts, axis=-1) * ng_ref[...] * (g * jax.nn.sigmoid(g))
    mix = _dot(o.astype(BF16), wog_ref[...]) + _dot(ona_ref[...], won_ref[...])
    x1 = _layer_norm(alpha * x_ref[...] + mix, g1_ref[...], b1_ref[...])
    _slab_store(x1_ref, x1)

    lane = lax.broadcasted_iota(jnp.int32, (tm, LANES), 1)
    logits = _dot(x1, wr_ref[...], precision=lax.Precision.HIGHEST) + br_ref[...]
    logits = jnp.where(lane < N_EXPERTS, logits, -jnp.inf)
    chosen = jnp.zeros((tm, LANES), F32)
    e_out = jnp.zeros((tm, LANES), jnp.int32)
    v_out = jnp.full((tm, LANES), -jnp.inf, F32)
    sels = []
    for k in range(TOP_K):
        m = jnp.max(logits, axis=-1, keepdims=True)
        idx = jnp.min(jnp.where(logits == m, lane, LANES), axis=-1, keepdims=True)
        sel = lane == idx
        sels.append(sel)
        e_out = jnp.where(lane == k, idx, e_out)
        v_out = jnp.where(lane == k, m, v_out)
        chosen = jnp.where(sel, 1.0, chosen)
        logits = jnp.where(sel, -jnp.inf, logits)
    ex = jnp.exp(v_out - jnp.max(v_out, axis=-1, keepdims=True))
    p_ref[...] = ex / jnp.sum(ex, axis=-1, keepdims=True)
    e_ref[...] = e_out
    before = _dot(lt_ref[...], chosen.astype(BF16)) + run_ref[...]
    pos = jnp.zeros((tm, LANES), jnp.int32)
    for k in range(TOP_K):
        pk = jnp.sum(jnp.where(sels[k], before, 0.0), axis=-1, keepdims=True)
        pos = jnp.where(lane == k, pk.astype(jnp.int32), pos)
    pos_ref[...] = pos
    run = run_ref[...] + jnp.sum(chosen, axis=0, keepdims=True)
    run_ref[...] = run
    cnt_ref[...] = jnp.broadcast_to(run, cnt_ref.shape).astype(jnp.int32)


def _outproj(o_f, o_b, gg, o_na, x, ng, wog, won, g1, b1, wr, br, *, alpha):
    n, d = x.shape
    tm = TOKEN_BLOCK
    row = lambda w: pl.BlockSpec((tm, w), lambda i: (i, 0))
    full = lambda a: pl.BlockSpec(a.shape, lambda i: (0,) * a.ndim)
    return pl.pallas_call(
        functools.partial(_outproj_kernel, alpha=alpha),
        grid=(n // tm,),
        in_specs=[row(GLA_V), row(GLA_V), row(GLA_V), row(NA_W), row(d), full(ng), full(wog), full(won),
                  full(g1), full(b1), full(wr), full(br)],
        out_specs=[pl.BlockSpec((tm * d // LANES, LANES), lambda i: (i, 0)), row(LANES), row(LANES), row(LANES),
                   pl.BlockSpec((8, LANES), lambda i: (0, 0))],
        out_shape=[jax.ShapeDtypeStruct((n * d // LANES, LANES), F32),
                   jax.ShapeDtypeStruct((n, LANES), jnp.int32), jax.ShapeDtypeStruct((n, LANES), F32),
                   jax.ShapeDtypeStruct((n, LANES), jnp.int32), jax.ShapeDtypeStruct((8, LANES), jnp.int32)],
        scratch_shapes=[pltpu.VMEM((1, LANES), F32), pltpu.VMEM((tm, tm), BF16)],
        compiler_params=_cparams("arbitrary"),
        name="outproj_router",
    )(o_f, o_b, gg, o_na, x, ng, wog, won, g1, b1, wr, br)


def _slab_rows(ref, row, per):
    return ref.at[pl.ds(pl.multiple_of(row * per, per), per)]


def _dispatch_kernel(dest_ref, x_ref, xs_in_ref, xs_ref, sem, *, per):
    del xs_in_ref
    tm = x_ref.shape[0] // per

    def copy(t, k):
        return pltpu.make_async_copy(_slab_rows(x_ref, t, per), _slab_rows(xs_ref, dest_ref[t * TOP_K + k], per), sem)

    def start(t, c):
        for k in range(TOP_K):
            copy(t, k).start()
        return c

    def wait(t, c):
        for k in range(TOP_K):
            copy(t, k).wait()
        return c

    lax.fori_loop(0, tm, start, 0)
    lax.fori_loop(0, tm, wait, 0)


def _dispatch(dest_flat, x1_slab, n_slots, per):
    n = x1_slab.shape[0] // per
    tm = TOKEN_BLOCK
    xs0 = jnp.zeros((n_slots * per, LANES), x1_slab.dtype)
    return pl.pallas_call(
        functools.partial(_dispatch_kernel, per=per),
        grid=(n // tm,),
        in_specs=[pl.BlockSpec((tm * TOP_K,), lambda i: (i,), memory_space=pltpu.SMEM),
                  pl.BlockSpec((tm * per, LANES), lambda i: (i, 0)),
                  pl.BlockSpec(memory_space=pl.ANY)],
        out_specs=pl.BlockSpec(memory_space=pl.ANY),
        out_shape=jax.ShapeDtypeStruct(xs0.shape, xs0.dtype),
        scratch_shapes=[pltpu.SemaphoreType.DMA(())],
        input_output_aliases={2: 0},
        compiler_params=_cparams("arbitrary"),
        name="moe_dispatch",
    )(dest_flat, x1_slab, xs0)


def _experts_kernel(be_ref, nact_ref, xs_ref, wgu_ref, bgu_ref, wd_ref, bd_ref, y_ref):
    i = pl.program_id(0)
    d, de = wd_ref.shape[2], wd_ref.shape[1]
    c = MOE_ROWS

    @pl.when(i < nact_ref[0])
    def _():
        x = _slab_load(xs_ref, c, d).astype(BF16)
        h = _dot(x, wgu_ref[0]) + bgu_ref[0]
        gate = jnp.minimum(h[:, :de], SWIGLU_LIMIT)
        up = jnp.clip(h[:, de:], -SWIGLU_LIMIT, SWIGLU_LIMIT)
        glu = gate * jax.nn.sigmoid(gate * SWIGLU_ALPHA)
        act = ((up + 1.0) * glu).astype(BF16)
        _slab_store(y_ref, _dot(act, wd_ref[0]) + bd_ref[0])

    @pl.when(i >= nact_ref[0])
    def _():
        y_ref[...] = jnp.zeros(y_ref.shape, y_ref.dtype)


def _experts(blk_e, nact, xs, wgu, bgu, wd, bd):
    d, de = wd.shape[2], wd.shape[1]
    per = d // LANES
    c = MOE_ROWS
    slab = pl.BlockSpec((c * per, LANES), lambda i, be, na: (i, 0))
    grid_spec = pltpu.PrefetchScalarGridSpec(
        num_scalar_prefetch=2,
        grid=(xs.shape[0] // (c * per),),
        in_specs=[slab,
                  pl.BlockSpec((1, d, 2 * de), lambda i, be, na: (be[i], 0, 0)),
                  pl.BlockSpec((1, 1, 2 * de), lambda i, be, na: (be[i], 0, 0)),
                  pl.BlockSpec((1, de, d), lambda i, be, na: (be[i], 0, 0)),
                  pl.BlockSpec((1, 1, d), lambda i, be, na: (be[i], 0, 0))],
        out_specs=slab,
    )
    return pl.pallas_call(
        _experts_kernel,
        grid_spec=grid_spec,
        out_shape=jax.ShapeDtypeStruct(xs.shape, F32),
        compiler_params=_cparams("arbitrary"),
        name="moe_experts",
    )(blk_e, nact, xs, wgu, bgu, wd, bd)


def _final_kernel(dest_ref, x1_ref, prob_ref, ple_ref, y_ref, wpp_ref, wpg_ref, bpg_ref, g2_ref, b2_ref,
                  out_ref, *scratch, alpha):
    ybufs, sem = scratch[:TOP_K], scratch[TOP_K]
    tm, d = out_ref.shape
    per = d // LANES

    def copy(t, k):
        return pltpu.make_async_copy(_slab_rows(y_ref, dest_ref[t * TOP_K + k], per), _slab_rows(ybufs[k], t, per), sem)

    def start(t, c):
        for k in range(TOP_K):
            copy(t, k).start()
        return c

    def wait(t, c):
        for k in range(TOP_K):
            copy(t, k).wait()
        return c

    lax.fori_loop(0, tm, start, 0)
    lax.fori_loop(0, tm, wait, 0)
    prob = prob_ref[...]
    moe = prob[:, 0:1] * _slab_load(ybufs[0], tm, d)
    for k in range(1, TOP_K):
        moe = moe + prob[:, k:k + 1] * _slab_load(ybufs[k], tm, d)
    r = alpha * _slab_load(x1_ref, tm, d) + moe
    gate = jax.nn.sigmoid(_dot(r.astype(BF16), wpg_ref[...]) + bpg_ref[...])
    u = _dot(ple_ref[...].astype(BF16), wpp_ref[...]) * gate
    out_ref[...] = _layer_norm(r + u, g2_ref[...], b2_ref[...])


def _final(dest_flat, x1_slab, prob, ple, y_slab, wpp, wpg, bpg, g2, b2, *, alpha):
    n = prob.shape[0]
    d = wpg.shape[0]
    per = d // LANES
    tm = FINAL_BLOCK
    row = lambda w: pl.BlockSpec((tm, w), lambda i: (i, 0))
    full = lambda a: pl.BlockSpec(a.shape, lambda i: (0,) * a.ndim)
    return pl.pallas_call(
        functools.partial(_final_kernel, alpha=alpha),
        grid=(n // tm,),
        in_specs=[pl.BlockSpec((tm * TOP_K,), lambda i: (i,), memory_space=pltpu.SMEM),
                  pl.BlockSpec((tm * per, LANES), lambda i: (i, 0)), row(LANES), row(ple.shape[1]),
                  pl.BlockSpec(memory_space=pl.ANY),
                  full(wpp), full(wpg), full(bpg), full(g2), full(b2)],
        out_specs=row(d),
        out_shape=jax.ShapeDtypeStruct((n, d), F32),
        scratch_shapes=[pltpu.VMEM((tm * per, LANES), F32)] * TOP_K + [pltpu.SemaphoreType.DMA(())],
        compiler_params=_cparams("arbitrary"),
        name="moe_combine_final",
    )(dest_flat, x1_slab, prob, ple, y_slab, wpp, wpg, bpg, g2, b2)


def _routing_tables(top_e, pos, counts, n_blocks):
    c = MOE_ROWS
    padded = (counts + c - 1) // c * c
    pad_end = jnp.cumsum(padded)
    pad_start = pad_end - padded
    dest = pad_start[top_e] + pos
    blk_e = jnp.minimum(jnp.searchsorted(pad_end, jnp.arange(n_blocks, dtype=jnp.int32) * c, side='right'),
                        N_EXPERTS - 1).astype(jnp.int32)
    nact = (pad_end[-1:] // c).astype(jnp.int32)
    return dest.reshape(-1).astype(jnp.int32), blk_e, nact


def _layer(x, ple, lw, groups, *, alpha, ln0):
    n, d = x.shape
    outs = _inproj(x, ln0[0], ln0[1], lw['wg'], lw['wlr'], lw['wgk'], lw['bgk'], lw['wn'], apply_ln=ln0[2])
    if ln0[2]:
        x, outs = outs[0], outs[1:]
    qk, v, gg, lg, qn, kn, vn = outs
    o_f, o_b = _gla(qk, v, lg, groups)
    o_na = _na(qn, kn, vn, lw['na_bias'], groups)
    x1, top_e, prob, pos, cnt = _outproj(o_f, o_b, gg, o_na, x, lw['ng'], lw['wog'], lw['won'],
                                         lw['g1'], lw['b1'], lw['wr'], lw['br'], alpha=alpha)
    n_blocks = (n * TOP_K + N_EXPERTS * (MOE_ROWS - 1) + MOE_ROWS - 1) // MOE_ROWS
    dest, blk_e, nact = _routing_tables(top_e[:, :TOP_K], pos[:, :TOP_K], cnt[0, :N_EXPERTS], n_blocks)
    xs = _dispatch(dest, x1, n_blocks * MOE_ROWS, d // LANES)
    y = _experts(blk_e, nact, xs, lw['wgu'], lw['bgu'], lw['wd'], lw['bd'])
    return _final(dest, x1, prob, ple, y, lw['wpp'], lw['wpg'], lw['bpg'], lw['g2'], lw['b2'], alpha=alpha)


def kernel(x_prompt, x_sample, p_prompt, p_sample, emb_ln_g, emb_ln_b, w_in, w_gk_f, b_gk_f, w_gk_b, b_gk_b, gla_norm_g, rpb, w_out, ln1_g, ln1_b, w_router, b_router, w_gu, b_gu, w_down, b_down, w_ple_proj, w_ple_gate, b_ple_gate, ln2_g, ln2_b):
    depth, d = w_in.shape[0], w_in.shape[1]
    alpha = float((2 * depth) ** 0.25)
    tb = TOKEN_BLOCK
    groups, first = [], 0
    for a in (x_prompt, x_sample):
        b, t, _ = a.shape
        assert t % tb == 0 and t % GRID_W == 0 and t // GRID_W >= NA_KH
        groups.append((first, t // tb))
        first += b * t // tb
    groups = tuple(groups)
    x = jnp.concatenate([x_prompt.reshape(-1, d), x_sample.reshape(-1, d)], axis=0)
    ple = jnp.concatenate([p_prompt.reshape(depth, -1, p_prompt.shape[-1]),
                           p_sample.reshape(depth, -1, p_sample.shape[-1])], axis=1)
    row = lambda a: a.reshape(1, -1).astype(F32)
    c0, c1, c2 = 2 * GLA_QK + 2 * GLA_V, 2 * GLA_QK + 2 * GLA_V + 2 * GLA_RANK, w_in.shape[2]
    for i in range(depth):
        wgk = jnp.zeros((LANES, 2 * GLA_QK), F32)
        wgk = wgk.at[:GLA_RANK, :GLA_QK].set(w_gk_f[i]).at[GLA_RANK:2 * GLA_RANK, GLA_QK:].set(w_gk_b[i])
        lw = dict(
            wg=w_in[i, :, :c0].astype(BF16),
            wlr=jnp.pad(w_in[i, :, c0:c1], ((0, 0), (0, LANES - 2 * GLA_RANK))).astype(BF16),
            wgk=wgk.astype(BF16),
            bgk=jnp.concatenate([b_gk_f[i], b_gk_b[i]]).reshape(1, -1),
            wn=w_in[i, :, c1:c2].astype(BF16),
            na_bias=_na_bias_table(rpb[i]),
            ng=row(gla_norm_g[i]),
            wog=w_out[i, :GLA_V].astype(BF16), won=w_out[i, GLA_V:].astype(BF16),
            g1=row(ln1_g[i]), b1=row(ln1_b[i]),
            wr=jnp.pad(w_router[i].astype(F32), ((0, 0), (0, LANES - N_EXPERTS))),
            br=jnp.pad(b_router[i].astype(F32), (0, LANES - N_EXPERTS)).reshape(1, -1),
            wgu=w_gu[i].astype(BF16), bgu=b_gu[i][:, None, :], wd=w_down[i].astype(BF16), bd=b_down[i][:, None, :],
            wpp=w_ple_proj[i].astype(BF16), wpg=w_ple_gate[i].astype(BF16), bpg=row(b_ple_gate[i]),
            g2=row(ln2_g[i]), b2=row(ln2_b[i]),
        )
        x = _layer(x, ple[i], lw, groups, alpha=alpha, ln0=(row(emb_ln_g), row(emb_ln_b), i == 0))
    n_p = x_prompt.shape[0] * x_prompt.shape[1]
    return (x[:n_p].reshape(x_prompt.shape), x[n_p:].reshape(x_sample.shape))
```

```python
import functools

import numpy as np
import jax
import jax.numpy as jnp
from jax import lax
from jax.experimental import pallas as pl
from jax.experimental.pallas import tpu as pltpu

GRID_W = 64
GLA_HEADS, GLA_DK, GLA_DV, GLA_RANK = 4, 64, 128, 16
GLA_NORMALIZER = 16.0
NA_HEADS, NA_DH, NA_KH, NA_KW = 8, 64, 8, 16
N_EXPERTS, TOP_K = 32, 4
SWIGLU_ALPHA, SWIGLU_LIMIT = 1.702, 7.0
LN_EPS, RMS_EPS = 1e-5, 1e-6

GLA_QK = GLA_HEADS * GLA_DK
GLA_V = GLA_HEADS * GLA_DV
NA_W = NA_HEADS * NA_DH

LANES = 128
V7X_VMEM_LIMIT_BYTES = 56 * 1024 * 1024

TOKEN_BLOCK = 512
GLA_CHUNK = 128
GLA_SAFE_LOG_DECAY = 40.0
NA_ROWS_PER_BLOCK = TOKEN_BLOCK // GRID_W
MOE_ROWS = 512
FINAL_BLOCK = 256
NEG_BIG = -1e30

F32 = jnp.float32
BF16 = jnp.bfloat16


def _cparams(*sem):
    return pltpu.CompilerParams(dimension_semantics=sem, vmem_limit_bytes=V7X_VMEM_LIMIT_BYTES)


def _dot(a, b, precision=None):
    return jnp.dot(a, b, preferred_element_type=F32, precision=precision)


def _dot_nt(a, b, precision=None):
    return lax.dot_general(a, b, (((1,), (1,)), ((), ())), preferred_element_type=F32, precision=precision)


def _dot_tn(a, b):
    return lax.dot_general(a, b, (((0,), (0,)), ((), ())), preferred_element_type=F32)


def _layer_norm(x, g, b):
    mu = jnp.mean(x, axis=-1, keepdims=True)
    xc = x - mu
    var = jnp.mean(xc * xc, axis=-1, keepdims=True)
    return xc * lax.rsqrt(var + LN_EPS) * g + b


def _slab_load(ref, n, width):
    per = width // LANES
    return jnp.concatenate([ref[pl.ds(s, n, stride=per), :] for s in range(per)], axis=-1)


def _slab_store(ref, x):
    n, width = x.shape
    per = width // LANES
    for s in range(per):
        ref[pl.ds(s, n, stride=per), :] = x[:, s * LANES:(s + 1) * LANES]


def _seq_local(blk, groups):
    loc = lax.rem(blk - groups[-1][0], groups[-1][1])
    bps = jnp.int32(groups[-1][1])
    for (first, per), nxt in zip(reversed(groups[:-1]), reversed(groups[1:])):
        inside = blk < nxt[0]
        loc = jnp.where(inside, lax.rem(blk - first, per), loc)
        bps = jnp.where(inside, per, bps)
    return loc, bps


def _part_specs(parts, block_rows):
    specs, firsts, first = [], [], 0
    for a in parts:
        nb = a.shape[0] // block_rows
        specs.append(pl.BlockSpec((block_rows, a.shape[1]),
                                  lambda i, first=first, nb=nb: (jnp.clip(i - first, 0, nb - 1), 0)))
        firsts.append(first)
        first += nb
    return specs, tuple(firsts)


def _select_part(i, refs, firsts):
    x = refs[0][...]
    for ref, first in zip(refs[1:], firsts[1:]):
        x = jnp.where(i >= first, ref[...], x)
    return x


def _inproj_kernel(*refs, firsts, apply_ln):
    x_refs, refs = refs[:len(firsts)], refs[len(firsts):]
    g_ref, b_ref, wg_ref, wlr_ref, wgk_ref, bgk_ref, wn_ref = refs[:7]
    out_refs = refs[7:]
    if apply_ln:
        x0_ref, qk_ref, v_ref, vt_ref, gg_ref, lg_ref, qn_ref, kn_ref, vn_ref = out_refs
    else:
        qk_ref, v_ref, vt_ref, gg_ref, lg_ref, qn_ref, kn_ref, vn_ref = out_refs
    x = _select_part(pl.program_id(0), x_refs, firsts)
    if apply_ln:
        x = _layer_norm(x, g_ref[...], b_ref[...])
        x0_ref[...] = x
    xb = x.astype(BF16)
    zg = _dot(xb, wg_ref[...])
    qk_ref[:, :GLA_QK] = (zg[:, :GLA_QK] * GLA_DK ** -0.5).astype(BF16)
    qk_ref[:, GLA_QK:] = zg[:, GLA_QK:2 * GLA_QK].astype(BF16)
    v = zg[:, 2 * GLA_QK:2 * GLA_QK + GLA_V]
    v_ref[...] = v.astype(BF16)
    vt_ref[...] = v.T.astype(BF16)
    gg_ref[...] = zg[:, 2 * GLA_QK + GLA_V:]
    lr = _dot(xb, wlr_ref[...])
    pre = _dot(lr.astype(BF16), wgk_ref[...]) + bgk_ref[...]
    log_sig = jnp.minimum(pre, 0.0) - jnp.log1p(jnp.exp(-jnp.abs(pre)))
    lg_ref[...] = log_sig * (1.0 / GLA_NORMALIZER)
    zn = _dot(xb, wn_ref[...])
    qn_ref[...] = (zn[:, :NA_W] * NA_DH ** -0.5).astype(BF16)
    kn_ref[...] = zn[:, NA_W:2 * NA_W].astype(BF16)
    vn_ref[...] = zn[:, 2 * NA_W:].astype(BF16)


def _inproj(x_parts, ln_g, ln_b, wg, wlr, wgk, bgk, wn, *, apply_ln):
    n = sum(a.shape[0] for a in x_parts)
    d = x_parts[0].shape[1]
    tm = TOKEN_BLOCK
    row = lambda w: pl.BlockSpec((tm, w), lambda i: (i, 0))
    full = lambda a: pl.BlockSpec(a.shape, lambda i: (0,) * a.ndim)
    x_specs, firsts = _part_specs(x_parts, tm)
    outs = [(GLA_QK * 2, BF16, False), (GLA_V, BF16, False), (GLA_V, BF16, True), (GLA_V, F32, False),
            (2 * GLA_QK, F32, False), (NA_W, BF16, False), (NA_W, BF16, False), (NA_W, BF16, False)]
    if apply_ln:
        outs = [(d, F32, False)] + outs
    return pl.pallas_call(
        functools.partial(_inproj_kernel, firsts=firsts, apply_ln=apply_ln),
        grid=(n // tm,),
        in_specs=x_specs + [full(ln_g), full(ln_b), full(wg), full(wlr), full(wgk), full(bgk), full(wn)],
        out_specs=[pl.BlockSpec((w, tm), lambda i: (0, i)) if tr else row(w) for w, _, tr in outs],
        out_shape=[jax.ShapeDtypeStruct((w, n) if tr else (n, w), t) for w, t, tr in outs],
        compiler_params=_cparams("arbitrary"),
        name="inproj_ln" if apply_ln else "inproj",
    )(*x_parts, ln_g, ln_b, wg, wlr, wgk, bgk, wn)


GLA_HEADS_PER_TILE = LANES // GLA_DK
assert GLA_DV == LANES and GLA_HEADS % GLA_HEADS_PER_TILE == 0


def _head_lane_masks(rows):
    lane = lax.broadcasted_iota(jnp.int32, (rows, LANES), 1)
    return [(lane >= hh * GLA_DK) & (lane < (hh + 1) * GLA_DK) for hh in range(GLA_HEADS_PER_TILE)]


def _gla_chunk_fast(qk_ref, v_ref, vt_ref, lg_ref, o_ref, s_ref, d, c, fwd):
    C = GLA_CHUNK
    rows = pl.ds(c * C, C)
    r_i = lax.broadcasted_iota(jnp.int32, (C, C), 0)
    c_i = lax.broadcasted_iota(jnp.int32, (C, C), 1)
    tri = ((r_i >= c_i) if fwd else (r_i <= c_i)).astype(BF16)
    lg = lg_ref[rows, :]
    lg1 = lg.astype(BF16)
    rem = lg - lg1.astype(F32)
    lg2 = rem.astype(BF16)
    lg3 = (rem - lg2.astype(F32)).astype(BF16)
    cum = _dot(tri, lg1) + (_dot(tri, lg2) + _dot(tri, lg3))
    yield
    tot = cum[C - 1:C, :] if fwd else cum[0:1, :]
    q = qk_ref[rows, :GLA_QK].astype(F32)
    k = qk_ref[rows, GLA_QK:].astype(F32)
    qt = (q * jnp.exp(cum)).astype(BF16)
    kt = (k * jnp.exp(-cum)).astype(BF16)
    kd = (k * jnp.exp(tot - cum)).astype(BF16)
    dec = jnp.exp(tot)
    mask = (r_i >= c_i) if fwd else (r_i < c_i)
    in_head = _head_lane_masks(C)
    zero = jnp.zeros((C, LANES), BF16)
    tiles = [slice(p * LANES, (p + 1) * LANES) for p in range(GLA_HEADS // GLA_HEADS_PER_TILE)]
    yield
    states, aos = [], []
    for p, tile in enumerate(tiles):
        s = s_ref[d, p]
        q_heads = jnp.concatenate([jnp.where(m, qt[:, tile], zero) for m in in_head], axis=0)
        aos.append(_dot_nt(q_heads, jnp.concatenate([kt[:, tile], s.astype(BF16)], axis=0)))
        states.append(s)
        yield
    for p, tile in enumerate(tiles):
        vt = [vt_ref[(p * GLA_HEADS_PER_TILE + hh) * GLA_DV:(p * GLA_HEADS_PER_TILE + hh + 1) * GLA_DV,
                     c * C:(c + 1) * C] for hh in range(GLA_HEADS_PER_TILE)]
        kd_heads = jnp.concatenate([jnp.where(m, kd[:, tile], zero) for m in in_head], axis=0)
        s_ref[d, p] = dec[:, tile] * states[p] + _dot(jnp.concatenate(vt, axis=1), kd_heads)
        yield
    for p in range(len(tiles)):
        for hh in range(GLA_HEADS_PER_TILE):
            h = p * GLA_HEADS_PER_TILE + hh
            vs = slice(h * GLA_DV, (h + 1) * GLA_DV)
            a = jnp.where(mask, aos[p][hh * C:(hh + 1) * C, :C], 0.0).astype(BF16)
            o_ref[rows, vs] = _dot(a, v_ref[rows, vs]) + aos[p][hh * C:(hh + 1) * C, C:]
            yield


GLA_STAGES_TO_STATE = 2 + 2 * (GLA_HEADS // GLA_HEADS_PER_TILE)


def _interleave(staggered):
    pending = sorted(staggered, key=lambda item: item[0])
    live, tick = [], 0
    while pending or live:
        while pending and pending[0][0] <= tick:
            live.append(pending.pop(0)[1])
        for g in list(live):
            try:
                next(g)
            except StopIteration:
                live.remove(g)
        tick += 1


def _gla_block_slow(qk_ref, v_ref, lg_ref, o_ref, s_ref, d, fwd, tb):
    G = 16

    def group(n, carry):
        rows = pl.ds(pl.multiple_of((n if fwd else tb // G - 1 - n) * G, G), G)
        gate = jnp.exp(lg_ref[rows, :])
        q = qk_ref[rows, :GLA_QK].astype(F32)
        k = qk_ref[rows, GLA_QK:]
        qs = (q if fwd else q * gate).astype(BF16)
        row = lax.broadcasted_iota(jnp.int32, (G, LANES), 0)
        in_head = _head_lane_masks(G)
        zero = jnp.zeros((G, LANES), BF16)
        for p in range(GLA_HEADS // GLA_HEADS_PER_TILE):
            tile = slice(p * LANES, (p + 1) * LANES)
            heads = [p * GLA_HEADS_PER_TILE + hh for hh in range(GLA_HEADS_PER_TILE)]
            vs = [slice(h * GLA_DV, (h + 1) * GLA_DV) for h in heads]
            v = [v_ref[rows, sl] for sl in vs]
            k_heads = [jnp.where(m, k[:, tile], zero) for m in in_head]
            s = s_ref[d, p]
            o = [jnp.zeros((G, GLA_DV), F32) for _ in heads]
            for r in (range(G) if fwd else reversed(range(G))):
                upd = None
                for hh in range(len(heads)):
                    term = _dot_tn(jnp.where(row == r, v[hh], zero), k_heads[hh])
                    upd = term if upd is None else upd + term
                s_new = gate[r:r + 1, tile] * s + upd
                s_read = (s_new if fwd else s).astype(BF16)
                for hh in range(len(heads)):
                    o[hh] = o[hh] + _dot_nt(jnp.where((row == r) & in_head[hh], qs[:, tile], zero), s_read)
                s = s_new
            for hh in range(len(heads)):
                o_ref[rows, vs[hh]] = o[hh]
            s_ref[d, p] = s
        return carry

    lax.fori_loop(0, tb // G, group, 0)


def _gla_kernel(qkf_ref, vf_ref, vtf_ref, lgf_ref, qkb_ref, vb_ref, vtb_ref, lgb_ref, of_ref, ob_ref, s_ref,
                *, groups, tb):
    i = pl.program_id(0)
    nblk = pl.num_programs(0)
    loc_f, _ = _seq_local(i, groups)
    loc_b, bps_b = _seq_local(nblk - 1 - i, groups)

    @pl.when(loc_f == 0)
    def _():
        s_ref[0] = jnp.zeros(s_ref.shape[1:], F32)

    @pl.when(loc_b == bps_b - 1)
    def _():
        s_ref[1] = jnp.zeros(s_ref.shape[1:], F32)

    nc = tb // GLA_CHUNK
    worst = jnp.float32(0.0)
    for ref in (lgf_ref, lgb_ref):
        for c in range(nc):
            tot = jnp.sum(ref[pl.ds(c * GLA_CHUNK, GLA_CHUNK), :], axis=0, keepdims=True)
            worst = jnp.minimum(worst, jnp.min(tot))
    safe = worst > -GLA_SAFE_LOG_DECAY

    @pl.when(safe)
    def _():
        work = []
        for c in range(nc):
            work.append((c * GLA_STAGES_TO_STATE,
                         _gla_chunk_fast(qkf_ref, vf_ref, vtf_ref, lgf_ref, of_ref, s_ref, 0, c, True)))
            work.append((c * GLA_STAGES_TO_STATE,
                         _gla_chunk_fast(qkb_ref, vb_ref, vtb_ref, lgb_ref, ob_ref, s_ref, 1, nc - 1 - c, False)))
        _interleave(work)

    @pl.when(jnp.logical_not(safe))
    def _():
        _gla_block_slow(qkf_ref, vf_ref, lgf_ref, of_ref, s_ref, 0, True, tb)
        _gla_block_slow(qkb_ref, vb_ref, lgb_ref, ob_ref, s_ref, 1, False, tb)


def _gla(qk, v, vt, lg, groups):
    n = qk.shape[0]
    tb = TOKEN_BLOCK
    nblk = n // tb
    fwd = lambda w, j: pl.BlockSpec((tb, w), lambda i: (i, j))
    bwd = lambda w, j: pl.BlockSpec((tb, w), lambda i: (nblk - 1 - i, j))
    return pl.pallas_call(
        functools.partial(_gla_kernel, groups=groups, tb=tb),
        grid=(nblk,),
        in_specs=[fwd(2 * GLA_QK, 0), fwd(GLA_V, 0), pl.BlockSpec((GLA_V, tb), lambda i: (0, i)), fwd(GLA_QK, 0),
                  bwd(2 * GLA_QK, 0), bwd(GLA_V, 0), pl.BlockSpec((GLA_V, tb), lambda i: (0, nblk - 1 - i)),
                  bwd(GLA_QK, 1)],
        out_specs=[fwd(GLA_V, 0), bwd(GLA_V, 0)],
        out_shape=[jax.ShapeDtypeStruct((n, GLA_V), F32)] * 2,
        scratch_shapes=[pltpu.VMEM((2, GLA_HEADS // GLA_HEADS_PER_TILE, GLA_DV, LANES), F32)],
        compiler_params=_cparams("arbitrary"),
        name="gla",
    )(qk, v, vt, lg, qk, v, vt, lg)


def _na_bias_table(rpb):
    w = GRID_W
    col = np.arange(w)
    start = np.clip(col - NA_KW // 2, 0, w - NA_KW)
    kc = np.arange(w)
    in_win = (kc[None, :] >= start[:, None]) & (kc[None, :] < start[:, None] + NA_KW)
    col_off = np.clip(kc[None, :] - col[:, None] + (NA_KW - 1), 0, 2 * NA_KW - 2)
    shift = np.arange(NA_KH)
    row_off = np.arange(NA_KH)[None, :] - shift[:, None] + (NA_KH - 1)
    t = rpb[:, row_off]
    t = t[:, :, :, col_off]
    t = jnp.where(jnp.asarray(in_win)[None, None, None], t, NEG_BIG)
    t = t.transpose(1, 0, 3, 2, 4)
    heads = LANES // NA_DH
    return t.reshape(NA_KH, NA_HEADS // heads, heads * w, NA_KH * w).astype(F32)


def _na_kernel(q_ref, kp_ref, kc_ref, kn_ref, vp_ref, vc_ref, vn_ref, bias_ref, o_ref, kbuf, vbuf, *, groups):
    tb = TOKEN_BLOCK
    w = GRID_W
    rpb_rows = NA_ROWS_PER_BLOCK
    blk = pl.program_id(1)
    loc, bps = _seq_local(blk, groups)
    rows_in_seq = bps * rpb_rows
    for j, (kr, vr) in enumerate(((kp_ref, vp_ref), (kc_ref, vc_ref), (kn_ref, vn_ref))):
        kbuf[pl.ds(j * tb, tb), :] = kr[...]
        vbuf[pl.ds(j * tb, tb), :] = vr[...]
    lane = lax.broadcasted_iota(jnp.int32, (w, LANES), 1)
    heads = LANES // NA_DH
    in_head = [(lane >= hh * NA_DH) & (lane < (hh + 1) * NA_DH) for hh in range(heads)]
    win = NA_KH * w

    def probs(j):
        r = loc * rpb_rows + j
        rs = jnp.clip(r - NA_KH // 2, 0, rows_in_seq - NA_KH)
        off = pl.multiple_of((rs - loc * rpb_rows + rpb_rows) * w, w)
        q = q_ref[pl.ds(j * w, w), :]
        q_heads = jnp.concatenate([jnp.where(m, q, jnp.zeros_like(q)) for m in in_head], axis=0)
        s = _dot_nt(q_heads, kbuf[pl.ds(off, win), :]) + bias_ref[r - rs, 0]
        p = jnp.exp(s - jnp.max(s, axis=-1, keepdims=True))
        return j, off, p.astype(BF16), jnp.sum(p, axis=-1, keepdims=True)

    def output(j, off, p, l):
        o = _dot(p, vbuf[pl.ds(off, win), :]) * (1.0 / l)
        out = o[:w]
        for hh in range(1, heads):
            out = jnp.where(in_head[hh], o[hh * w:(hh + 1) * w], out)
        o_ref[pl.ds(j * w, w), :] = out.astype(o_ref.dtype)

    pending = []
    for j in range(rpb_rows):
        pending.append(probs(j))
        if len(pending) > 2:
            output(*pending.pop(0))
    for item in pending:
        output(*item)


def _na(q, k, v, bias, groups):
    n = q.shape[0]
    tb = TOKEN_BLOCK
    nblk = n // tb
    nhp = NA_W // LANES
    cur = pl.BlockSpec((tb, LANES), lambda hp, i: (i, hp))
    prv = pl.BlockSpec((tb, LANES), lambda hp, i: (jnp.maximum(i - 1, 0), hp))
    nxt = pl.BlockSpec((tb, LANES), lambda hp, i: (jnp.minimum(i + 1, nblk - 1), hp))
    return pl.pallas_call(
        functools.partial(_na_kernel, groups=groups),
        grid=(nhp, nblk),
        in_specs=[cur, prv, cur, nxt, prv, cur, nxt,
                  pl.BlockSpec((NA_KH, 1) + bias.shape[2:], lambda hp, i: (0, hp, 0, 0))],
        out_specs=cur,
        out_shape=jax.ShapeDtypeStruct((n, NA_W), BF16),
        scratch_shapes=[pltpu.VMEM((3 * tb, LANES), BF16), pltpu.VMEM((3 * tb, LANES), BF16)],
        compiler_params=_cparams("parallel", "parallel"),
        name="natten",
    )(q, k, k, k, v, v, v, bias)


def _outproj_kernel(of_ref, ob_ref, gg_ref, ona_ref, x_ref, ng_ref, wog_ref, won_ref, g1_ref, b1_ref,
                    wrh_ref, wrl_ref, br_ref, x1_ref, e_ref, p_ref, pos_ref, cnt_ref, run_ref, ut_ref,
                    *, alpha):
    i = pl.program_id(0)
    tm = x_ref.shape[0]

    @pl.when(i == 0)
    def _():
        run_ref[...] = jnp.zeros(run_ref.shape, F32)
        r_i = lax.broadcasted_iota(jnp.int32, (tm, tm), 0)
        c_i = lax.broadcasted_iota(jnp.int32, (tm, tm), 1)
        ut_ref[...] = (r_i < c_i).astype(BF16)

    o = of_ref[...] + ob_ref[...]
    parts = []
    for h in range(GLA_HEADS):
        oh = o[:, h * GLA_DV:(h + 1) * GLA_DV]
        parts.append(oh * lax.rsqrt(jnp.mean(oh * oh, axis=-1, keepdims=True) + RMS_EPS))
    g = gg_ref[...]
    o = jnp.concatenate(parts, axis=-1) * ng_ref[...] * (g * jax.nn.sigmoid(g))
    mix = _dot(o.astype(BF16), wog_ref[...]) + _dot(ona_ref[...], won_ref[...])
    x1 = _layer_norm(alpha * x_ref[...] + mix, g1_ref[...], b1_ref[...])
    _slab_store(x1_ref, x1)

    x_hi = x1.astype(BF16)
    x_lo = (x1 - x_hi.astype(F32)).astype(BF16)
    logits = (_dot(x_hi, wrh_ref[...]) + (_dot(x_hi, wrl_ref[...]) + _dot(x_lo, wrh_ref[...])) + br_ref[...])
    lg = logits.T[:N_EXPERTS]
    eid = lax.broadcasted_iota(jnp.int32, (N_EXPERTS, tm), 0)
    out_row = lax.broadcasted_iota(jnp.int32, (8, tm), 0)
    chosen = jnp.zeros((N_EXPERTS, tm), F32)
    e_out = jnp.zeros((8, tm), jnp.int32)
    vals, sels = [], []
    for k in range(TOP_K):
        m = jnp.max(lg, axis=0, keepdims=True)
        idx = jnp.min(jnp.where(lg == m, eid, N_EXPERTS), axis=0, keepdims=True)
        sel = eid == idx
        sels.append(sel)
        vals.append(m)
        e_out = jnp.where(out_row == k, idx, e_out)
        chosen = jnp.where(sel, 1.0, chosen)
        lg = jnp.where(sel, -jnp.inf, lg)
    ex = [jnp.exp(v - vals[0]) for v in vals]
    inv = 1.0 / functools.reduce(lambda a, b: a + b, ex)
    p_out = jnp.zeros((8, tm), F32)
    for k in range(TOP_K):
        p_out = jnp.where(out_row == k, ex[k] * inv, p_out)
    p_ref[...] = p_out
    e_ref[...] = e_out
    run = run_ref[...]
    before = _dot(chosen.astype(BF16), ut_ref[...]) + run[:, :1]
    pos = jnp.zeros((8, tm), jnp.int32)
    for k in range(TOP_K):
        pk = jnp.sum(jnp.where(sels[k], before, 0.0), axis=0, keepdims=True)
        pos = jnp.where(out_row == k, pk.astype(jnp.int32), pos)
    pos_ref[...] = pos
    run = run + jnp.sum(chosen, axis=1, keepdims=True)
    run_ref[...] = run
    cnt_ref[...] = run.astype(jnp.int32)


def _outproj(o_f, o_b, gg, o_na, x, ng, wog, won, g1, b1, wrh, wrl, br, *, alpha):
    n, d = x.shape
    tm = TOKEN_BLOCK
    row = lambda w: pl.BlockSpec((tm, w), lambda i: (i, 0))
    col = pl.BlockSpec((8, tm), lambda i: (0, i))
    full = lambda a: pl.BlockSpec(a.shape, lambda i: (0,) * a.ndim)
    return pl.pallas_call(
        functools.partial(_outproj_kernel, alpha=alpha),
        grid=(n // tm,),
        in_specs=[row(GLA_V), row(GLA_V), row(GLA_V), row(NA_W), row(d), full(ng), full(wog), full(won),
                  full(g1), full(b1), full(wrh), full(wrl), full(br)],
        out_specs=[pl.BlockSpec((tm * d // LANES, LANES), lambda i: (i, 0)), col, col, col,
                   pl.BlockSpec((N_EXPERTS, LANES), lambda i: (0, 0))],
        out_shape=[jax.ShapeDtypeStruct((n * d // LANES, LANES), F32),
                   jax.ShapeDtypeStruct((8, n), jnp.int32), jax.ShapeDtypeStruct((8, n), F32),
                   jax.ShapeDtypeStruct((8, n), jnp.int32), jax.ShapeDtypeStruct((N_EXPERTS, LANES), jnp.int32)],
        scratch_shapes=[pltpu.VMEM((N_EXPERTS, LANES), F32), pltpu.VMEM((tm, tm), BF16)],
        compiler_params=_cparams("arbitrary"),
        name="outproj_router",
    )(o_f, o_b, gg, o_na, x, ng, wog, won, g1, b1, wrh, wrl, br)


def _slab_rows(ref, row, per):
    return ref.at[pl.ds(pl.multiple_of(row * per, per), per)]


DMA_ISSUE_UNROLL = 8


def _dispatch_kernel(dest_ref, x_ref, xs_ref, sem, *, per):
    tm = x_ref.shape[0] // per

    def start(t, c):
        for k in range(TOP_K):
            pltpu.make_async_copy(_slab_rows(x_ref, t, per),
                                  _slab_rows(xs_ref, dest_ref[t * TOP_K + k], per), sem).start()
        return c

    lax.fori_loop(0, tm, start, 0, unroll=DMA_ISSUE_UNROLL)
    for k in range(TOP_K):
        pltpu.make_async_copy(x_ref, xs_ref.at[pl.ds(0, tm * per)], sem).wait()


def _dispatch(dest_flat, x1_slab, n_slots, per):
    n = x1_slab.shape[0] // per
    tm = TOKEN_BLOCK
    return pl.pallas_call(
        functools.partial(_dispatch_kernel, per=per),
        grid=(n // tm,),
        in_specs=[pl.BlockSpec((tm * TOP_K,), lambda i: (i,), memory_space=pltpu.SMEM),
                  pl.BlockSpec((tm * per, LANES), lambda i: (i, 0))],
        out_specs=pl.BlockSpec(memory_space=pl.ANY),
        out_shape=jax.ShapeDtypeStruct((n_slots * per, LANES), x1_slab.dtype),
        scratch_shapes=[pltpu.SemaphoreType.DMA(())],
        compiler_params=_cparams("arbitrary"),
        name="moe_dispatch",
    )(dest_flat, x1_slab)


def _experts_kernel(be_ref, nact_ref, xs_ref, wgu_ref, bgu_ref, wd_ref, bd_ref, y_ref):
    i = pl.program_id(0)
    d, de = wd_ref.shape[2], wd_ref.shape[1]
    c = MOE_ROWS

    @pl.when(i < nact_ref[0])
    def _():
        x = _slab_load(xs_ref, c, d).astype(BF16)
        h = _dot(x, wgu_ref[0]) + bgu_ref[0]
        gate = jnp.minimum(h[:, :de], SWIGLU_LIMIT)
        up = jnp.clip(h[:, de:], -SWIGLU_LIMIT, SWIGLU_LIMIT)
        glu = gate * jax.nn.sigmoid(gate * SWIGLU_ALPHA)
        act = ((up + 1.0) * glu).astype(BF16)
        _slab_store(y_ref, _dot(act, wd_ref[0]) + bd_ref[0])

    @pl.when(i >= nact_ref[0])
    def _():
        y_ref[...] = jnp.zeros(y_ref.shape, y_ref.dtype)


def _experts(blk_e, nact, xs, wgu, bgu, wd, bd):
    d, de = wd.shape[2], wd.shape[1]
    per = d // LANES
    c = MOE_ROWS
    slab = pl.BlockSpec((c * per, LANES), lambda i, be, na: (i, 0))
    grid_spec = pltpu.PrefetchScalarGridSpec(
        num_scalar_prefetch=2,
        grid=(xs.shape[0] // (c * per),),
        in_specs=[slab,
                  pl.BlockSpec((1, d, 2 * de), lambda i, be, na: (be[i], 0, 0)),
                  pl.BlockSpec((1, 1, 2 * de), lambda i, be, na: (be[i], 0, 0)),
                  pl.BlockSpec((1, de, d), lambda i, be, na: (be[i], 0, 0)),
                  pl.BlockSpec((1, 1, d), lambda i, be, na: (be[i], 0, 0))],
        out_specs=slab,
    )
    return pl.pallas_call(
        _experts_kernel,
        grid_spec=grid_spec,
        out_shape=jax.ShapeDtypeStruct(xs.shape, F32),
        compiler_params=_cparams("arbitrary"),
        name="moe_experts",
    )(blk_e, nact, xs, wgu, bgu, wd, bd)


def _final_kernel(*refs, ple_firsts, out_firsts, alpha):
    dest_ref, dest_next_ref, x1_ref, prob_ref = refs[:4]
    refs = refs[4:]
    ple_refs, refs = refs[:len(ple_firsts)], refs[len(ple_firsts):]
    y_ref, wpp_ref, wpg_ref, bpg_ref, g2_ref, b2_ref = refs[:6]
    refs = refs[6:]
    out_refs, scratch = refs[:len(out_firsts)], refs[len(out_firsts):]
    ybufs, sems = scratch[:TOP_K], scratch[TOP_K]
    tm, d = out_refs[0].shape
    per = d // LANES
    i = pl.program_id(0)
    slot = lax.rem(i, 2)

    def gather(idx_ref, into):
        def start(t, c):
            for k in range(TOP_K):
                pltpu.make_async_copy(_slab_rows(y_ref, idx_ref[t * TOP_K + k], per),
                                      _slab_rows(ybufs[k], into * tm + t, per), sems.at[into]).start()
            return c
        lax.fori_loop(0, tm, start, 0, unroll=DMA_ISSUE_UNROLL)

    @pl.when(i == 0)
    def _():
        gather(dest_ref, slot)

    @pl.when(i + 1 < pl.num_programs(0))
    def _():
        gather(dest_next_ref, 1 - slot)

    half = pl.ds(pl.multiple_of(slot * (tm * per), tm * per), tm * per)
    for k in range(TOP_K):
        pltpu.make_async_copy(y_ref.at[pl.ds(0, tm * per)], ybufs[k].at[half], sems.at[slot]).wait()
    prob = prob_ref[...]
    moe = None
    for k in range(TOP_K):
        yk = _slab_load(ybufs[k].at[half], tm, d)
        moe = prob[:, k:k + 1] * yk if moe is None else moe + prob[:, k:k + 1] * yk
    r = alpha * _slab_load(x1_ref, tm, d) + moe
    gate = jax.nn.sigmoid(_dot(r.astype(BF16), wpg_ref[...]) + bpg_ref[...])
    ple = _select_part(i, ple_refs, ple_firsts)
    u = _dot(ple.astype(BF16), wpp_ref[...]) * gate
    res = _layer_norm(r + u, g2_ref[...], b2_ref[...])
    if len(out_refs) == 1:
        out_refs[0][...] = res
    else:
        bounds = list(out_firsts[1:]) + [pl.num_programs(0)]
        for ref, first, end in zip(out_refs, out_firsts, bounds):
            @pl.when((i >= first) & (i < end))
            def _(ref=ref):
                ref[...] = res


def _final(dest_flat, x1_slab, prob, ple_parts, y_slab, wpp, wpg, bpg, g2, b2, out_rows, *, alpha):
    n = prob.shape[0]
    d = wpg.shape[0]
    per = d // LANES
    tm = FINAL_BLOCK
    nblk = n // tm
    row = lambda w: pl.BlockSpec((tm, w), lambda i: (i, 0))
    full = lambda a: pl.BlockSpec(a.shape, lambda i: (0,) * a.ndim)
    ple_specs, ple_firsts = _part_specs(ple_parts, tm)
    out_shapes = [jax.ShapeDtypeStruct((r, d), F32) for r in out_rows]
    out_specs, out_firsts = _part_specs(out_shapes, tm)
    return pl.pallas_call(
        functools.partial(_final_kernel, ple_firsts=ple_firsts, out_firsts=out_firsts, alpha=alpha),
        grid=(nblk,),
        in_specs=[pl.BlockSpec((tm * TOP_K,), lambda i: (i,), memory_space=pltpu.SMEM),
                  pl.BlockSpec((tm * TOP_K,), lambda i: (jnp.minimum(i + 1, nblk - 1),), memory_space=pltpu.SMEM),
                  pl.BlockSpec((tm * per, LANES), lambda i: (i, 0)), row(prob.shape[1])]
                 + ple_specs
                 + [pl.BlockSpec(memory_space=pl.ANY), full(wpp), full(wpg), full(bpg), full(g2), full(b2)],
        out_specs=out_specs,
        out_shape=out_shapes,
        scratch_shapes=[pltpu.VMEM((2 * tm * per, LANES), F32)] * TOP_K + [pltpu.SemaphoreType.DMA((2,))],
        compiler_params=_cparams("arbitrary"),
        name="moe_combine_final",
    )(dest_flat, dest_flat, x1_slab, prob, *ple_parts, y_slab, wpp, wpg, bpg, g2, b2)


def _routing_tables(top_e, pos, counts, n_blocks):
    c = MOE_ROWS
    e = jnp.arange(N_EXPERTS, dtype=jnp.int32)
    padded = (counts + c - 1) // c * c
    pad_end = jnp.sum(jnp.where(e[:, None] <= e[None, :], padded[:, None], 0), axis=0)
    pad_start = pad_end - padded
    start_of = jnp.sum(jnp.where(top_e[None] == e[:, None, None], pad_start[:, None, None], 0), axis=0)
    dest = (start_of + pos).T.reshape(-1)
    blk_first = jnp.arange(n_blocks, dtype=jnp.int32) * c
    blk_e = jnp.minimum(jnp.sum((pad_end[None, :] <= blk_first[:, None]).astype(jnp.int32), axis=1),
                        N_EXPERTS - 1)
    nact = pad_end[-1:] // c
    return dest.astype(jnp.int32), blk_e.astype(jnp.int32), nact.astype(jnp.int32)


def _layer(x_parts, ple_parts, lw, groups, out_rows, *, alpha, ln0):
    d = x_parts[0].shape[1]
    outs = _inproj(x_parts, ln0[0], ln0[1], lw['wg'], lw['wlr'], lw['wgk'], lw['bgk'], lw['wn'], apply_ln=ln0[2])
    if ln0[2]:
        x, outs = outs[0], outs[1:]
    else:
        (x,) = x_parts
    n = x.shape[0]
    qk, v, vt, gg, lg, qn, kn, vn = outs
    o_f, o_b = _gla(qk, v, vt, lg, groups)
    o_na = _na(qn, kn, vn, lw['na_bias'], groups)
    x1, top_e, prob, pos, cnt = _outproj(o_f, o_b, gg, o_na, x, lw['ng'], lw['wog'], lw['won'],
                                         lw['g1'], lw['b1'], lw['wrh'], lw['wrl'], lw['br'], alpha=alpha)
    n_blocks = (n * TOP_K + N_EXPERTS * (MOE_ROWS - 1) + MOE_ROWS - 1) // MOE_ROWS
    dest, blk_e, nact = _routing_tables(top_e[:TOP_K], pos[:TOP_K], cnt[:, 0], n_blocks)
    xs = _dispatch(dest, x1, n_blocks * MOE_ROWS, d // LANES)
    y = _experts(blk_e, nact, xs, lw['wgu'], lw['bgu'], lw['wd'], lw['bd'])
    return _final(dest, x1, prob.T, ple_parts, y, lw['wpp'], lw['wpg'], lw['bpg'], lw['g2'], lw['b2'],
                  out_rows, alpha=alpha)


def kernel(x_prompt, x_sample, p_prompt, p_sample, emb_ln_g, emb_ln_b, w_in, w_gk_f, b_gk_f, w_gk_b, b_gk_b, gla_norm_g, rpb, w_out, ln1_g, ln1_b, w_router, b_router, w_gu, b_gu, w_down, b_down, w_ple_proj, w_ple_gate, b_ple_gate, ln2_g, ln2_b):
    depth, d = w_in.shape[0], w_in.shape[1]
    alpha = float((2 * depth) ** 0.25)
    tb = TOKEN_BLOCK
    groups, first = [], 0
    for a in (x_prompt, x_sample):
        b, t, _ = a.shape
        assert t % tb == 0 and t % GRID_W == 0 and t // GRID_W >= NA_KH
        groups.append((first, t // tb))
        first += b * t // tb
    groups = tuple(groups)
    x_parts = [x_prompt.reshape(-1, d), x_sample.reshape(-1, d)]
    ple_parts = [p_prompt.reshape(depth, -1, p_prompt.shape[-1]), p_sample.reshape(depth, -1, p_sample.shape[-1])]
    n_all = sum(a.shape[0] for a in x_parts)
    row = lambda a: a.reshape(1, -1).astype(F32)
    c0, c1, c2 = 2 * GLA_QK + 2 * GLA_V, 2 * GLA_QK + 2 * GLA_V + 2 * GLA_RANK, w_in.shape[2]
    for i in range(depth):
        wgk = jnp.zeros((LANES, 2 * GLA_QK), F32)
        wgk = wgk.at[:GLA_RANK, :GLA_QK].set(w_gk_f[i]).at[GLA_RANK:2 * GLA_RANK, GLA_QK:].set(w_gk_b[i])
        wr = jnp.pad(w_router[i].astype(F32), ((0, 0), (0, LANES - N_EXPERTS)))
        lw = dict(
            wg=w_in[i, :, :c0].astype(BF16),
            wlr=jnp.pad(w_in[i, :, c0:c1], ((0, 0), (0, LANES - 2 * GLA_RANK))).astype(BF16),
            wgk=wgk.astype(BF16),
            bgk=jnp.concatenate([b_gk_f[i], b_gk_b[i]]).reshape(1, -1),
            wn=w_in[i, :, c1:c2].astype(BF16),
            na_bias=_na_bias_table(rpb[i]),
            ng=row(gla_norm_g[i]),
            wog=w_out[i, :GLA_V].astype(BF16), won=w_out[i, GLA_V:].astype(BF16),
            g1=row(ln1_g[i]), b1=row(ln1_b[i]),
            wrh=wr.astype(BF16), wrl=(wr - wr.astype(BF16).astype(F32)).astype(BF16),
            br=jnp.pad(b_router[i].astype(F32), (0, LANES - N_EXPERTS)).reshape(1, -1),
            wgu=w_gu[i].astype(BF16), bgu=b_gu[i][:, None, :], wd=w_down[i].astype(BF16), bd=b_down[i][:, None, :],
            wpp=w_ple_proj[i].astype(BF16), wpg=w_ple_gate[i].astype(BF16), bpg=row(b_ple_gate[i]),
            g2=row(ln2_g[i]), b2=row(ln2_b[i]),
        )
        last = i == depth - 1
        out_rows = [a.shape[0] for a in x_parts] if last else [n_all]
        outs = _layer(x_parts if i == 0 else [x], [p[i] for p in ple_parts], lw, groups, out_rows,
                      alpha=alpha, ln0=(row(emb_ln_g), row(emb_ln_b), i == 0))
        if not last:
            (x,) = outs
    return (outs[0].reshape(x_prompt.shape), outs[1].reshape(x_sample.shape))
```

```python
import functools

import numpy as np
import jax
import jax.numpy as jnp
from jax import lax
from jax.experimental import pallas as pl
from jax.experimental.pallas import tpu as pltpu

GRID_W = 64
GLA_HEADS, GLA_DK, GLA_DV, GLA_RANK = 4, 64, 128, 16
GLA_NORMALIZER = 16.0
NA_HEADS, NA_DH, NA_KH, NA_KW = 8, 64, 8, 16
N_EXPERTS, TOP_K = 32, 4
SWIGLU_ALPHA, SWIGLU_LIMIT = 1.702, 7.0
LN_EPS, RMS_EPS = 1e-5, 1e-6

GLA_QK = GLA_HEADS * GLA_DK
GLA_V = GLA_HEADS * GLA_DV
NA_W = NA_HEADS * NA_DH

LANES = 128
V7X_VMEM_LIMIT_BYTES = 56 * 1024 * 1024

TOKEN_BLOCK = 512
GLA_CHUNK = 128
GLA_SAFE_LOG_DECAY = 40.0
NA_ROWS_PER_BLOCK = TOKEN_BLOCK // GRID_W
MOE_ROWS = 512
FINAL_BLOCK = 256
NEG_BIG = -1e30

F32 = jnp.float32
BF16 = jnp.bfloat16


def _cparams(*sem):
    return pltpu.CompilerParams(dimension_semantics=sem, vmem_limit_bytes=V7X_VMEM_LIMIT_BYTES)


def _dot(a, b, precision=None):
    return jnp.dot(a, b, preferred_element_type=F32, precision=precision)


def _dot_nt(a, b, precision=None):
    return lax.dot_general(a, b, (((1,), (1,)), ((), ())), preferred_element_type=F32, precision=precision)


def _dot_tn(a, b):
    return lax.dot_general(a, b, (((0,), (0,)), ((), ())), preferred_element_type=F32)


def _layer_norm(x, g, b):
    mu = jnp.mean(x, axis=-1, keepdims=True)
    xc = x - mu
    var = jnp.mean(xc * xc, axis=-1, keepdims=True)
    return xc * lax.rsqrt(var + LN_EPS) * g + b


def _slab_load(ref, n, width):
    per = width // LANES
    return jnp.concatenate([ref[pl.ds(s, n, stride=per), :] for s in range(per)], axis=-1)


def _slab_store(ref, x):
    n, width = x.shape
    per = width // LANES
    for s in range(per):
        ref[pl.ds(s, n, stride=per), :] = x[:, s * LANES:(s + 1) * LANES]


def _seq_local(blk, groups):
    loc = lax.rem(blk - groups[-1][0], groups[-1][1])
    bps = jnp.int32(groups[-1][1])
    for (first, per), nxt in zip(reversed(groups[:-1]), reversed(groups[1:])):
        inside = blk < nxt[0]
        loc = jnp.where(inside, lax.rem(blk - first, per), loc)
        bps = jnp.where(inside, per, bps)
    return loc, bps


def _part_specs(parts, block_rows):
    specs, firsts, first = [], [], 0
    for a in parts:
        nb = a.shape[0] // block_rows
        specs.append(pl.BlockSpec((block_rows, a.shape[1]),
                                  lambda i, first=first, nb=nb: (jnp.clip(i - first, 0, nb - 1), 0)))
        firsts.append(first)
        first += nb
    return specs, tuple(firsts)


def _select_part(i, refs, firsts):
    x = refs[0][...]
    for ref, first in zip(refs[1:], firsts[1:]):
        x = jnp.where(i >= first, ref[...], x)
    return x


def _inproj_kernel(*refs, firsts, apply_ln):
    x_refs, refs = refs[:len(firsts)], refs[len(firsts):]
    g_ref, b_ref, wg_ref, wlr_ref, wgk_ref, bgk_ref, wn_ref = refs[:7]
    out_refs = refs[7:]
    if apply_ln:
        x0_ref, qk_ref, v_ref, vt_ref, gg_ref, lg_ref, qn_ref, kn_ref, vn_ref = out_refs
    else:
        qk_ref, v_ref, vt_ref, gg_ref, lg_ref, qn_ref, kn_ref, vn_ref = out_refs
    x = _select_part(pl.program_id(0), x_refs, firsts)
    if apply_ln:
        x = _layer_norm(x, g_ref[...], b_ref[...])
        x0_ref[...] = x
    xb = x.astype(BF16)
    zg = _dot(xb, wg_ref[...])
    qk_ref[:, :GLA_QK] = (zg[:, :GLA_QK] * GLA_DK ** -0.5).astype(BF16)
    qk_ref[:, GLA_QK:] = zg[:, GLA_QK:2 * GLA_QK].astype(BF16)
    v = zg[:, 2 * GLA_QK:2 * GLA_QK + GLA_V]
    v_ref[...] = v.astype(BF16)
    vt_ref[...] = v.T.astype(BF16)
    gg_ref[...] = zg[:, 2 * GLA_QK + GLA_V:]
    lr = _dot(xb, wlr_ref[...])
    pre = _dot(lr.astype(BF16), wgk_ref[...]) + bgk_ref[...]
    log_sig = jnp.minimum(pre, 0.0) - jnp.log1p(jnp.exp(-jnp.abs(pre)))
    lg_ref[...] = log_sig * (1.0 / GLA_NORMALIZER)
    zn = _dot(xb, wn_ref[...])
    qn_ref[...] = (zn[:, :NA_W] * NA_DH ** -0.5).astype(BF16)
    kn_ref[...] = zn[:, NA_W:2 * NA_W].astype(BF16)
    vn_ref[...] = zn[:, 2 * NA_W:].astype(BF16)


def _inproj(x_parts, ln_g, ln_b, wg, wlr, wgk, bgk, wn, *, apply_ln):
    n = sum(a.shape[0] for a in x_parts)
    d = x_parts[0].shape[1]
    tm = TOKEN_BLOCK
    row = lambda w: pl.BlockSpec((tm, w), lambda i: (i, 0))
    full = lambda a: pl.BlockSpec(a.shape, lambda i: (0,) * a.ndim)
    x_specs, firsts = _part_specs(x_parts, tm)
    outs = [(GLA_QK * 2, BF16, False), (GLA_V, BF16, False), (GLA_V, BF16, True), (GLA_V, F32, False),
            (2 * GLA_QK, F32, False), (NA_W, BF16, False), (NA_W, BF16, False), (NA_W, BF16, False)]
    if apply_ln:
        outs = [(d, F32, False)] + outs
    return pl.pallas_call(
        functools.partial(_inproj_kernel, firsts=firsts, apply_ln=apply_ln),
        grid=(n // tm,),
        in_specs=x_specs + [full(ln_g), full(ln_b), full(wg), full(wlr), full(wgk), full(bgk), full(wn)],
        out_specs=[pl.BlockSpec((w, tm), lambda i: (0, i)) if tr else row(w) for w, _, tr in outs],
        out_shape=[jax.ShapeDtypeStruct((w, n) if tr else (n, w), t) for w, t, tr in outs],
        compiler_params=_cparams("arbitrary"),
        name="inproj_ln" if apply_ln else "inproj",
    )(*x_parts, ln_g, ln_b, wg, wlr, wgk, bgk, wn)


GLA_HEADS_PER_TILE = LANES // GLA_DK
assert GLA_DV == LANES and GLA_HEADS % GLA_HEADS_PER_TILE == 0


def _head_lane_masks(rows):
    lane = lax.broadcasted_iota(jnp.int32, (rows, LANES), 1)
    return [(lane >= hh * GLA_DK) & (lane < (hh + 1) * GLA_DK) for hh in range(GLA_HEADS_PER_TILE)]


def _gla_chunk_fast(qk_ref, v_ref, vt_ref, lg_ref, o_ref, s_ref, d, c, fwd):
    C = GLA_CHUNK
    rows = pl.ds(c * C, C)
    r_i = lax.broadcasted_iota(jnp.int32, (C, C), 0)
    c_i = lax.broadcasted_iota(jnp.int32, (C, C), 1)
    tri = ((r_i >= c_i) if fwd else (r_i <= c_i)).astype(BF16)
    lg = lg_ref[rows, :]
    lg1 = lg.astype(BF16)
    rem = lg - lg1.astype(F32)
    lg2 = rem.astype(BF16)
    lg3 = (rem - lg2.astype(F32)).astype(BF16)
    cum = _dot(tri, lg1) + (_dot(tri, lg2) + _dot(tri, lg3))
    yield
    tot = cum[C - 1:C, :] if fwd else cum[0:1, :]
    q = qk_ref[rows, :GLA_QK].astype(F32)
    k = qk_ref[rows, GLA_QK:].astype(F32)
    qt = (q * jnp.exp(cum)).astype(BF16)
    kt = (k * jnp.exp(-cum)).astype(BF16)
    kd = (k * jnp.exp(tot - cum)).astype(BF16)
    dec = jnp.exp(tot)
    mask = (r_i >= c_i) if fwd else (r_i < c_i)
    in_head = _head_lane_masks(C)
    zero = jnp.zeros((C, LANES), BF16)
    tiles = [slice(p * LANES, (p + 1) * LANES) for p in range(GLA_HEADS // GLA_HEADS_PER_TILE)]
    yield
    states, aos = [], []
    for p, tile in enumerate(tiles):
        s = s_ref[d, p]
        q_heads = jnp.concatenate([jnp.where(m, qt[:, tile], zero) for m in in_head], axis=0)
        aos.append(_dot_nt(q_heads, jnp.concatenate([kt[:, tile], s.astype(BF16)], axis=0)))
        states.append(s)
        yield
    for p, tile in enumerate(tiles):
        vt = [vt_ref[(p * GLA_HEADS_PER_TILE + hh) * GLA_DV:(p * GLA_HEADS_PER_TILE + hh + 1) * GLA_DV,
                     c * C:(c + 1) * C] for hh in range(GLA_HEADS_PER_TILE)]
        kd_heads = jnp.concatenate([jnp.where(m, kd[:, tile], zero) for m in in_head], axis=0)
        s_ref[d, p] = dec[:, tile] * states[p] + _dot(jnp.concatenate(vt, axis=1), kd_heads)
        yield
    for p in range(len(tiles)):
        for hh in range(GLA_HEADS_PER_TILE):
            h = p * GLA_HEADS_PER_TILE + hh
            vs = slice(h * GLA_DV, (h + 1) * GLA_DV)
            a = jnp.where(mask, aos[p][hh * C:(hh + 1) * C, :C], 0.0).astype(BF16)
            o_ref[rows, vs] = _dot(a, v_ref[rows, vs]) + aos[p][hh * C:(hh + 1) * C, C:]
            yield


GLA_STAGES_TO_STATE = 2 + 2 * (GLA_HEADS // GLA_HEADS_PER_TILE)


def _interleave(staggered):
    pending = sorted(staggered, key=lambda item: item[0])
    live, tick = [], 0
    while pending or live:
        while pending and pending[0][0] <= tick:
            live.append(pending.pop(0)[1])
        for g in list(live):
            try:
                next(g)
            except StopIteration:
                live.remove(g)
        tick += 1


def _gla_block_slow(qk_ref, v_ref, lg_ref, o_ref, s_ref, d, fwd, tb):
    G = 16

    def group(n, carry):
        rows = pl.ds(pl.multiple_of((n if fwd else tb // G - 1 - n) * G, G), G)
        gate = jnp.exp(lg_ref[rows, :])
        q = qk_ref[rows, :GLA_QK].astype(F32)
        k = qk_ref[rows, GLA_QK:]
        qs = (q if fwd else q * gate).astype(BF16)
        row = lax.broadcasted_iota(jnp.int32, (G, LANES), 0)
        in_head = _head_lane_masks(G)
        zero = jnp.zeros((G, LANES), BF16)
        for p in range(GLA_HEADS // GLA_HEADS_PER_TILE):
            tile = slice(p * LANES, (p + 1) * LANES)
            heads = [p * GLA_HEADS_PER_TILE + hh for hh in range(GLA_HEADS_PER_TILE)]
            vs = [slice(h * GLA_DV, (h + 1) * GLA_DV) for h in heads]
            v = [v_ref[rows, sl] for sl in vs]
            k_heads = [jnp.where(m, k[:, tile], zero) for m in in_head]
            s = s_ref[d, p]
            o = [jnp.zeros((G, GLA_DV), F32) for _ in heads]
            for r in (range(G) if fwd else reversed(range(G))):
                upd = None
                for hh in range(len(heads)):
                    term = _dot_tn(jnp.where(row == r, v[hh], zero), k_heads[hh])
                    upd = term if upd is None else upd + term
                s_new = gate[r:r + 1, tile] * s + upd
                s_read = (s_new if fwd else s).astype(BF16)
                for hh in range(len(heads)):
                    o[hh] = o[hh] + _dot_nt(jnp.where((row == r) & in_head[hh], qs[:, tile], zero), s_read)
                s = s_new
            for hh in range(len(heads)):
                o_ref[rows, vs[hh]] = o[hh]
            s_ref[d, p] = s
        return carry

    lax.fori_loop(0, tb // G, group, 0)


def _gla_kernel(qkf_ref, vf_ref, vtf_ref, lgf_ref, qkb_ref, vb_ref, vtb_ref, lgb_ref, of_ref, ob_ref, s_ref,
                *, groups, tb):
    i = pl.program_id(0)
    nblk = pl.num_programs(0)
    loc_f, _ = _seq_local(i, groups)
    loc_b, bps_b = _seq_local(nblk - 1 - i, groups)

    @pl.when(loc_f == 0)
    def _():
        s_ref[0] = jnp.zeros(s_ref.shape[1:], F32)

    @pl.when(loc_b == bps_b - 1)
    def _():
        s_ref[1] = jnp.zeros(s_ref.shape[1:], F32)

    nc = tb // GLA_CHUNK
    worst = jnp.float32(0.0)
    for ref in (lgf_ref, lgb_ref):
        for c in range(nc):
            tot = jnp.sum(ref[pl.ds(c * GLA_CHUNK, GLA_CHUNK), :], axis=0, keepdims=True)
            worst = jnp.minimum(worst, jnp.min(tot))
    safe = worst > -GLA_SAFE_LOG_DECAY

    @pl.when(safe)
    def _():
        work = []
        for c in range(nc):
            work.append((c * GLA_STAGES_TO_STATE,
                         _gla_chunk_fast(qkf_ref, vf_ref, vtf_ref, lgf_ref, of_ref, s_ref, 0, c, True)))
            work.append((c * GLA_STAGES_TO_STATE,
                         _gla_chunk_fast(qkb_ref, vb_ref, vtb_ref, lgb_ref, ob_ref, s_ref, 1, nc - 1 - c, False)))
        _interleave(work)

    @pl.when(jnp.logical_not(safe))
    def _():
        _gla_block_slow(qkf_ref, vf_ref, lgf_ref, of_ref, s_ref, 0, True, tb)
        _gla_block_slow(qkb_ref, vb_ref, lgb_ref, ob_ref, s_ref, 1, False, tb)


def _gla(qk, v, vt, lg, groups):
    n = qk.shape[0]
    tb = TOKEN_BLOCK
    nblk = n // tb
    fwd = lambda w, j: pl.BlockSpec((tb, w), lambda i: (i, j))
    bwd = lambda w, j: pl.BlockSpec((tb, w), lambda i: (nblk - 1 - i, j))
    return pl.pallas_call(
        functools.partial(_gla_kernel, groups=groups, tb=tb),
        grid=(nblk,),
        in_specs=[fwd(2 * GLA_QK, 0), fwd(GLA_V, 0), pl.BlockSpec((GLA_V, tb), lambda i: (0, i)), fwd(GLA_QK, 0),
                  bwd(2 * GLA_QK, 0), bwd(GLA_V, 0), pl.BlockSpec((GLA_V, tb), lambda i: (0, nblk - 1 - i)),
                  bwd(GLA_QK, 1)],
        out_specs=[fwd(GLA_V, 0), bwd(GLA_V, 0)],
        out_shape=[jax.ShapeDtypeStruct((n, GLA_V), F32)] * 2,
        scratch_shapes=[pltpu.VMEM((2, GLA_HEADS // GLA_HEADS_PER_TILE, GLA_DV, LANES), F32)],
        compiler_params=_cparams("arbitrary"),
        name="gla",
    )(qk, v, vt, lg, qk, v, vt, lg)


def _na_bias_table(rpb):
    w = GRID_W
    col = np.arange(w)
    start = np.clip(col - NA_KW // 2, 0, w - NA_KW)
    kc = np.arange(w)
    in_win = (kc[None, :] >= start[:, None]) & (kc[None, :] < start[:, None] + NA_KW)
    col_off = np.clip(kc[None, :] - col[:, None] + (NA_KW - 1), 0, 2 * NA_KW - 2)
    shift = np.arange(NA_KH)
    row_off = np.arange(NA_KH)[None, :] - shift[:, None] + (NA_KH - 1)
    t = rpb[:, row_off]
    t = t[:, :, :, col_off]
    t = jnp.where(jnp.asarray(in_win)[None, None, None], t, NEG_BIG)
    t = t.transpose(1, 0, 3, 2, 4)
    heads = LANES // NA_DH
    return t.reshape(NA_KH, NA_HEADS // heads, heads * w, NA_KH * w).astype(F32)


def _na_kernel(q_ref, kp_ref, kc_ref, kn_ref, vp_ref, vc_ref, vn_ref, bias_ref, o_ref, kbuf, vbuf, *, groups):
    tb = TOKEN_BLOCK
    w = GRID_W
    rpb_rows = NA_ROWS_PER_BLOCK
    blk = pl.program_id(1)
    loc, bps = _seq_local(blk, groups)
    rows_in_seq = bps * rpb_rows
    for j, (kr, vr) in enumerate(((kp_ref, vp_ref), (kc_ref, vc_ref), (kn_ref, vn_ref))):
        kbuf[pl.ds(j * tb, tb), :] = kr[...]
        vbuf[pl.ds(j * tb, tb), :] = vr[...]
    lane = lax.broadcasted_iota(jnp.int32, (w, LANES), 1)
    heads = LANES // NA_DH
    in_head = [(lane >= hh * NA_DH) & (lane < (hh + 1) * NA_DH) for hh in range(heads)]
    win = NA_KH * w

    def probs(j):
        r = loc * rpb_rows + j
        rs = jnp.clip(r - NA_KH // 2, 0, rows_in_seq - NA_KH)
        off = pl.multiple_of((rs - loc * rpb_rows + rpb_rows) * w, w)
        q = q_ref[pl.ds(j * w, w), :]
        q_heads = jnp.concatenate([jnp.where(m, q, jnp.zeros_like(q)) for m in in_head], axis=0)
        s = _dot_nt(q_heads, kbuf[pl.ds(off, win), :]) + bias_ref[r - rs, 0]
        p = jnp.exp(s - jnp.max(s, axis=-1, keepdims=True))
        return j, off, p.astype(BF16), jnp.sum(p, axis=-1, keepdims=True)

    def output(j, off, p, l):
        o = _dot(p, vbuf[pl.ds(off, win), :]) * (1.0 / l)
        out = o[:w]
        for hh in range(1, heads):
            out = jnp.where(in_head[hh], o[hh * w:(hh + 1) * w], out)
        o_ref[pl.ds(j * w, w), :] = out.astype(o_ref.dtype)

    pending = []
    for j in range(rpb_rows):
        pending.append(probs(j))
        if len(pending) > 2:
            output(*pending.pop(0))
    for item in pending:
        output(*item)


def _na(q, k, v, bias, groups):
    n = q.shape[0]
    tb = TOKEN_BLOCK
    nblk = n // tb
    nhp = NA_W // LANES
    cur = pl.BlockSpec((tb, LANES), lambda hp, i: (i, hp))
    prv = pl.BlockSpec((tb, LANES), lambda hp, i: (jnp.maximum(i - 1, 0), hp))
    nxt = pl.BlockSpec((tb, LANES), lambda hp, i: (jnp.minimum(i + 1, nblk - 1), hp))
    return pl.pallas_call(
        functools.partial(_na_kernel, groups=groups),
        grid=(nhp, nblk),
        in_specs=[cur, prv, cur, nxt, prv, cur, nxt,
                  pl.BlockSpec((NA_KH, 1) + bias.shape[2:], lambda hp, i: (0, hp, 0, 0))],
        out_specs=cur,
        out_shape=jax.ShapeDtypeStruct((n, NA_W), BF16),
        scratch_shapes=[pltpu.VMEM((3 * tb, LANES), BF16), pltpu.VMEM((3 * tb, LANES), BF16)],
        compiler_params=_cparams("parallel", "parallel"),
        name="natten",
    )(q, k, k, k, v, v, v, bias)


def _outproj_kernel(of_ref, ob_ref, gg_ref, ona_ref, x_ref, ng_ref, wog_ref, won_ref, g1_ref, b1_ref,
                    wrh_ref, wrl_ref, br_ref, x1_ref, e_ref, p_ref, pos_ref, cnt_ref, run_ref, ut_ref,
                    *, alpha):
    i = pl.program_id(0)
    tm = x_ref.shape[0]

    @pl.when(i == 0)
    def _():
        run_ref[...] = jnp.zeros(run_ref.shape, F32)
        r_i = lax.broadcasted_iota(jnp.int32, (tm, tm), 0)
        c_i = lax.broadcasted_iota(jnp.int32, (tm, tm), 1)
        ut_ref[...] = (r_i < c_i).astype(BF16)

    o = of_ref[...] + ob_ref[...]
    parts = []
    for h in range(GLA_HEADS):
        oh = o[:, h * GLA_DV:(h + 1) * GLA_DV]
        parts.append(oh * lax.rsqrt(jnp.mean(oh * oh, axis=-1, keepdims=True) + RMS_EPS))
    g = gg_ref[...]
    o = jnp.concatenate(parts, axis=-1) * ng_ref[...] * (g * jax.nn.sigmoid(g))
    mix = _dot(o.astype(BF16), wog_ref[...]) + _dot(ona_ref[...], won_ref[...])
    x1 = _layer_norm(alpha * x_ref[...] + mix, g1_ref[...], b1_ref[...])
    _slab_store(x1_ref, x1)

    x_hi = x1.astype(BF16)
    x_lo = (x1 - x_hi.astype(F32)).astype(BF16)
    logits = (_dot(x_hi, wrh_ref[...]) + (_dot(x_hi, wrl_ref[...]) + _dot(x_lo, wrh_ref[...])) + br_ref[...])
    lg = logits.T[:N_EXPERTS]
    eid = lax.broadcasted_iota(jnp.int32, (N_EXPERTS, tm), 0)
    out_row = lax.broadcasted_iota(jnp.int32, (8, tm), 0)
    chosen = jnp.zeros((N_EXPERTS, tm), F32)
    e_out = jnp.zeros((8, tm), jnp.int32)
    vals, sels = [], []
    for k in range(TOP_K):
        m = jnp.max(lg, axis=0, keepdims=True)
        idx = jnp.min(jnp.where(lg == m, eid, N_EXPERTS), axis=0, keepdims=True)
        sel = eid == idx
        sels.append(sel)
        vals.append(m)
        e_out = jnp.where(out_row == k, idx, e_out)
        chosen = jnp.where(sel, 1.0, chosen)
        lg = jnp.where(sel, -jnp.inf, lg)
    ex = [jnp.exp(v - vals[0]) for v in vals]
    inv = 1.0 / functools.reduce(lambda a, b: a + b, ex)
    p_out = jnp.zeros((8, tm), F32)
    for k in range(TOP_K):
        p_out = jnp.where(out_row == k, ex[k] * inv, p_out)
    p_ref[...] = p_out
    e_ref[...] = e_out
    run = run_ref[...]
    before = _dot(chosen.astype(BF16), ut_ref[...]) + run[:, :1]
    pos = jnp.zeros((8, tm), jnp.int32)
    for k in range(TOP_K):
        pk = jnp.sum(jnp.where(sels[k], before, 0.0), axis=0, keepdims=True)
        pos = jnp.where(out_row == k, pk.astype(jnp.int32), pos)
    pos_ref[...] = pos
    run = run + jnp.sum(chosen, axis=1, keepdims=True)
    run_ref[...] = run
    cnt_ref[...] = run.astype(jnp.int32)


def _outproj(o_f, o_b, gg, o_na, x, ng, wog, won, g1, b1, wrh, wrl, br, *, alpha):
    n, d = x.shape
    tm = TOKEN_BLOCK
    row = lambda w: pl.BlockSpec((tm, w), lambda i: (i, 0))
    col = pl.BlockSpec((8, tm), lambda i: (0, i))
    full = lambda a: pl.BlockSpec(a.shape, lambda i: (0,) * a.ndim)
    return pl.pallas_call(
        functools.partial(_outproj_kernel, alpha=alpha),
        grid=(n // tm,),
        in_specs=[row(GLA_V), row(GLA_V), row(GLA_V), row(NA_W), row(d), full(ng), full(wog), full(won),
                  full(g1), full(b1), full(wrh), full(wrl), full(br)],
        out_specs=[pl.BlockSpec((tm * d // LANES, LANES), lambda i: (i, 0)), col, col, col,
                   pl.BlockSpec((N_EXPERTS, LANES), lambda i: (0, 0))],
        out_shape=[jax.ShapeDtypeStruct((n * d // LANES, LANES), F32),
                   jax.ShapeDtypeStruct((8, n), jnp.int32), jax.ShapeDtypeStruct((8, n), F32),
                   jax.ShapeDtypeStruct((8, n), jnp.int32), jax.ShapeDtypeStruct((N_EXPERTS, LANES), jnp.int32)],
        scratch_shapes=[pltpu.VMEM((N_EXPERTS, LANES), F32), pltpu.VMEM((tm, tm), BF16)],
        compiler_params=_cparams("arbitrary"),
        name="outproj_router",
    )(o_f, o_b, gg, o_na, x, ng, wog, won, g1, b1, wrh, wrl, br)


def _slab_rows(ref, row, per):
    return ref.at[pl.ds(pl.multiple_of(row * per, per), per)]


DMA_ISSUE_UNROLL = 8


def _experts_kernel(be_ref, nact_ref, src_next_ref, src_ref, dst_ref, x_ref, wgu_ref, bgu_ref, wd_ref, bd_ref,
                    y_ref, xbuf, ybuf, gsem, ssem):
    i = pl.program_id(0)
    last = pl.num_programs(0) - 1
    nact = nact_ref[0]
    d, de = wd_ref.shape[2], wd_ref.shape[1]
    per = d // LANES
    c = MOE_ROWS
    slot = lax.rem(i, 2)

    def half(buf, which):
        return buf.at[pl.ds(pl.multiple_of(which * (c * per), c * per), c * per)]

    def gather(idx_ref, into):
        def group(g, carry):
            for u in range(DMA_ISSUE_UNROLL):
                r = g * DMA_ISSUE_UNROLL + u
                pltpu.make_async_copy(_slab_rows(x_ref, idx_ref[r], per),
                                      _slab_rows(xbuf, into * c + r, per), gsem.at[into]).start(priority=u % 2)
            return carry
        lax.fori_loop(0, c // DMA_ISSUE_UNROLL, group, 0)

    def scatter(frm):
        def group(g, carry):
            for u in range(DMA_ISSUE_UNROLL):
                r = g * DMA_ISSUE_UNROLL + u
                pltpu.make_async_copy(_slab_rows(ybuf, frm * c + r, per),
                                      _slab_rows(y_ref, dst_ref[r], per), ssem.at[frm]).start(priority=u % 2)
            return carry
        lax.fori_loop(0, c // DMA_ISSUE_UNROLL, group, 0)

    def wait_block(buf, sem, which):
        pltpu.make_async_copy(y_ref.at[pl.ds(0, c * per)], half(buf, which), sem.at[which]).wait()

    @pl.when(i == 0)
    def _():
        gather(src_ref, slot)

    @pl.when(i + 1 < nact)
    def _():
        gather(src_next_ref, 1 - slot)

    @pl.when(i < nact)
    def _():
        wait_block(xbuf, gsem, slot)
        x = _slab_load(half(xbuf, slot), c, d).astype(BF16)
        h = _dot(x, wgu_ref[0]) + bgu_ref[0]
        gate = jnp.minimum(h[:, :de], SWIGLU_LIMIT)
        up = jnp.clip(h[:, de:], -SWIGLU_LIMIT, SWIGLU_LIMIT)
        glu = gate * jax.nn.sigmoid(gate * SWIGLU_ALPHA)
        act = ((up + 1.0) * glu).astype(BF16)
        _slab_store(half(ybuf, slot), _dot(act, wd_ref[0]) + bd_ref[0])
        scatter(slot)

    @pl.when((i >= 1) & (i - 1 < nact))
    def _():
        wait_block(ybuf, ssem, 1 - slot)

    @pl.when((i == last) & (i < nact))
    def _():
        wait_block(ybuf, ssem, slot)


def _experts(blk_e, nact, src_tok, dst_row, x1_slab, n_rows_out, wgu, bgu, wd, bd):
    d, de = wd.shape[2], wd.shape[1]
    per = d // LANES
    c = MOE_ROWS
    n_blocks = src_tok.shape[0] // c
    smem = lambda f: pl.BlockSpec((c,), f, memory_space=pltpu.SMEM)
    grid_spec = pltpu.PrefetchScalarGridSpec(
        num_scalar_prefetch=2,
        grid=(n_blocks,),
        in_specs=[smem(lambda i, be, na: (jnp.minimum(i + 1, n_blocks - 1),)),
                  smem(lambda i, be, na: (i,)),
                  smem(lambda i, be, na: (i,)),
                  pl.BlockSpec(memory_space=pl.ANY),
                  pl.BlockSpec((1, d, 2 * de), lambda i, be, na: (be[i], 0, 0)),
                  pl.BlockSpec((1, 1, 2 * de), lambda i, be, na: (be[i], 0, 0)),
                  pl.BlockSpec((1, de, d), lambda i, be, na: (be[i], 0, 0)),
                  pl.BlockSpec((1, 1, d), lambda i, be, na: (be[i], 0, 0))],
        out_specs=pl.BlockSpec(memory_space=pl.ANY),
        scratch_shapes=[pltpu.VMEM((2 * c * per, LANES), F32), pltpu.VMEM((2 * c * per, LANES), F32),
                        pltpu.SemaphoreType.DMA((2,)), pltpu.SemaphoreType.DMA((2,))],
    )
    return pl.pallas_call(
        _experts_kernel,
        grid_spec=grid_spec,
        out_shape=jax.ShapeDtypeStruct((n_rows_out * per, LANES), F32),
        compiler_params=_cparams("arbitrary"),
        name="moe_experts",
    )(blk_e, nact, src_tok, src_tok, dst_row, x1_slab, wgu, bgu, wd, bd)


def _final_kernel(*refs, ple_firsts, out_firsts, alpha):
    x1_ref, prob_ref = refs[:2]
    y_refs, refs = refs[2:2 + TOP_K], refs[2 + TOP_K:]
    ple_refs, refs = refs[:len(ple_firsts)], refs[len(ple_firsts):]
    wpp_ref, wpg_ref, bpg_ref, g2_ref, b2_ref = refs[:5]
    out_refs = refs[5:]
    tm, d = out_refs[0].shape
    i = pl.program_id(0)
    prob = prob_ref[...]
    moe = None
    for k in range(TOP_K):
        yk = _slab_load(y_refs[k], tm, d)
        moe = prob[:, k:k + 1] * yk if moe is None else moe + prob[:, k:k + 1] * yk
    r = alpha * _slab_load(x1_ref, tm, d) + moe
    gate = jax.nn.sigmoid(_dot(r.astype(BF16), wpg_ref[...]) + bpg_ref[...])
    ple = _select_part(i, ple_refs, ple_firsts)
    u = _dot(ple.astype(BF16), wpp_ref[...]) * gate
    res = _layer_norm(r + u, g2_ref[...], b2_ref[...])
    if len(out_refs) == 1:
        out_refs[0][...] = res
    else:
        bounds = list(out_firsts[1:]) + [pl.num_programs(0)]
        for ref, first, end in zip(out_refs, out_firsts, bounds):
            @pl.when((i >= first) & (i < end))
            def _(ref=ref):
                ref[...] = res


def _final(x1_slab, prob, ple_parts, y_slab, wpp, wpg, bpg, g2, b2, out_rows, *, alpha):
    n = prob.shape[0]
    d = wpg.shape[0]
    per = d // LANES
    tm = FINAL_BLOCK
    nblk = n // tm
    row = lambda w: pl.BlockSpec((tm, w), lambda i: (i, 0))
    slab = lambda k: pl.BlockSpec((tm * per, LANES), lambda i, k=k: (k * nblk + i, 0))
    full = lambda a: pl.BlockSpec(a.shape, lambda i: (0,) * a.ndim)
    ple_specs, ple_firsts = _part_specs(ple_parts, tm)
    out_shapes = [jax.ShapeDtypeStruct((r, d), F32) for r in out_rows]
    out_specs, out_firsts = _part_specs(out_shapes, tm)
    return pl.pallas_call(
        functools.partial(_final_kernel, ple_firsts=ple_firsts, out_firsts=out_firsts, alpha=alpha),
        grid=(nblk,),
        in_specs=[slab(0), row(prob.shape[1])] + [slab(k) for k in range(TOP_K)] + ple_specs
                 + [full(wpp), full(wpg), full(bpg), full(g2), full(b2)],
        out_specs=out_specs,
        out_shape=out_shapes,
        compiler_params=_cparams("arbitrary"),
        name="moe_combine_final",
    )(x1_slab, prob, *([y_slab] * TOP_K), *ple_parts, wpp, wpg, bpg, g2, b2)


def _routing_tables(top_e, pos, counts, n_blocks):
    c = MOE_ROWS
    n = top_e.shape[1]
    n_slots = n_blocks * c
    e = jnp.arange(N_EXPERTS, dtype=jnp.int32)
    padded = (counts + c - 1) // c * c
    pad_end = jnp.sum(jnp.where(e[:, None] <= e[None, :], padded[:, None], 0), axis=0)
    pad_start = pad_end - padded
    start_of = jnp.sum(jnp.where(top_e[None] == e[:, None, None], pad_start[:, None, None], 0), axis=0)
    slot_of = (start_of + pos).reshape(-1)
    spare = TOP_K * n + jnp.arange(n_slots, dtype=jnp.int32)
    dst_row = spare.at[slot_of].set(jnp.arange(TOP_K * n, dtype=jnp.int32))
    src_tok = jnp.where(dst_row < TOP_K * n, lax.rem(dst_row, n), 0)
    blk_first = jnp.arange(n_blocks, dtype=jnp.int32) * c
    blk_e = jnp.minimum(jnp.sum((pad_end[None, :] <= blk_first[:, None]).astype(jnp.int32), axis=1),
                        N_EXPERTS - 1)
    nact = pad_end[-1:] // c
    return src_tok.astype(jnp.int32), dst_row.astype(jnp.int32), blk_e.astype(jnp.int32), nact.astype(jnp.int32)


def _layer(x_parts, ple_parts, lw, groups, out_rows, *, alpha, ln0):
    d = x_parts[0].shape[1]
    outs = _inproj(x_parts, ln0[0], ln0[1], lw['wg'], lw['wlr'], lw['wgk'], lw['bgk'], lw['wn'], apply_ln=ln0[2])
    if ln0[2]:
        x, outs = outs[0], outs[1:]
    else:
        (x,) = x_parts
    n = x.shape[0]
    qk, v, vt, gg, lg, qn, kn, vn = outs
    o_f, o_b = _gla(qk, v, vt, lg, groups)
    o_na = _na(qn, kn, vn, lw['na_bias'], groups)
    x1, top_e, prob, pos, cnt = _outproj(o_f, o_b, gg, o_na, x, lw['ng'], lw['wog'], lw['won'],
                                         lw['g1'], lw['b1'], lw['wrh'], lw['wrl'], lw['br'], alpha=alpha)
    n_blocks = (n * TOP_K + N_EXPERTS * (MOE_ROWS - 1) + MOE_ROWS - 1) // MOE_ROWS
    src_tok, dst_row, blk_e, nact = _routing_tables(top_e[:TOP_K], pos[:TOP_K], cnt[:, 0], n_blocks)
    y = _experts(blk_e, nact, src_tok, dst_row, x1, TOP_K * n + n_blocks * MOE_ROWS,
                 lw['wgu'], lw['bgu'], lw['wd'], lw['bd'])
    return _final(x1, prob.T, ple_parts, y, lw['wpp'], lw['wpg'], lw['bpg'], lw['g2'], lw['b2'],
                  out_rows, alpha=alpha)


def kernel(x_prompt, x_sample, p_prompt, p_sample, emb_ln_g, emb_ln_b, w_in, w_gk_f, b_gk_f, w_gk_b, b_gk_b, gla_norm_g, rpb, w_out, ln1_g, ln1_b, w_router, b_router, w_gu, b_gu, w_down, b_down, w_ple_proj, w_ple_gate, b_ple_gate, ln2_g, ln2_b):
    depth, d = w_in.shape[0], w_in.shape[1]
    alpha = float((2 * depth) ** 0.25)
    tb = TOKEN_BLOCK
    groups, first = [], 0
    for a in (x_prompt, x_sample):
        b, t, _ = a.shape
        assert t % tb == 0 and t % GRID_W == 0 and t // GRID_W >= NA_KH
        groups.append((first, t // tb))
        first += b * t // tb
    groups = tuple(groups)
    x_parts = [x_prompt.reshape(-1, d), x_sample.reshape(-1, d)]
    ple_parts = [p_prompt.reshape(depth, -1, p_prompt.shape[-1]), p_sample.reshape(depth, -1, p_sample.shape[-1])]
    n_all = sum(a.shape[0] for a in x_parts)
    row = lambda a: a.reshape(1, -1).astype(F32)
    c0, c1, c2 = 2 * GLA_QK + 2 * GLA_V, 2 * GLA_QK + 2 * GLA_V + 2 * GLA_RANK, w_in.shape[2]
    for i in range(depth):
        wgk = jnp.zeros((LANES, 2 * GLA_QK), F32)
        wgk = wgk.at[:GLA_RANK, :GLA_QK].set(w_gk_f[i]).at[GLA_RANK:2 * GLA_RANK, GLA_QK:].set(w_gk_b[i])
        wr = jnp.pad(w_router[i].astype(F32), ((0, 0), (0, LANES - N_EXPERTS)))
        lw = dict(
            wg=w_in[i, :, :c0].astype(BF16),
            wlr=jnp.pad(w_in[i, :, c0:c1], ((0, 0), (0, LANES - 2 * GLA_RANK))).astype(BF16),
            wgk=wgk.astype(BF16),
            bgk=jnp.concatenate([b_gk_f[i], b_gk_b[i]]).reshape(1, -1),
            wn=w_in[i, :, c1:c2].astype(BF16),
            na_bias=_na_bias_table(rpb[i]),
            ng=row(gla_norm_g[i]),
            wog=w_out[i, :GLA_V].astype(BF16), won=w_out[i, GLA_V:].astype(BF16),
            g1=row(ln1_g[i]), b1=row(ln1_b[i]),
            wrh=wr.astype(BF16), wrl=(wr - wr.astype(BF16).astype(F32)).astype(BF16),
            br=jnp.pad(b_router[i].astype(F32), (0, LANES - N_EXPERTS)).reshape(1, -1),
            wgu=w_gu[i].astype(BF16), bgu=b_gu[i][:, None, :], wd=w_down[i].astype(BF16), bd=b_down[i][:, None, :],
            wpp=w_ple_proj[i].astype(BF16), wpg=w_ple_gate[i].astype(BF16), bpg=row(b_ple_gate[i]),
            g2=row(ln2_g[i]), b2=row(ln2_b[i]),
        )
        last = i == depth - 1
        out_rows = [a.shape[0] for a in x_parts] if last else [n_all]
        outs = _layer(x_parts if i == 0 else [x], [p[i] for p in ple_parts], lw, groups, out_rows,
                      alpha=alpha, ln0=(row(emb_ln_g), row(emb_ln_b), i == 0))
        if not last:
            (x,) = outs
    return (outs[0].reshape(x_prompt.shape), outs[1].reshape(x_sample.shape))
```

```python
import functools

import numpy as np
import jax
import jax.numpy as jnp
from jax import lax
from jax.experimental import pallas as pl
from jax.experimental.pallas import tpu as pltpu

GRID_W = 64
GLA_HEADS, GLA_DK, GLA_DV, GLA_RANK = 4, 64, 128, 16
GLA_NORMALIZER = 16.0
NA_HEADS, NA_DH, NA_KH, NA_KW = 8, 64, 8, 16
N_EXPERTS, TOP_K = 32, 4
SWIGLU_ALPHA, SWIGLU_LIMIT = 1.702, 7.0
LN_EPS, RMS_EPS = 1e-5, 1e-6

GLA_QK = GLA_HEADS * GLA_DK
GLA_V = GLA_HEADS * GLA_DV
NA_W = NA_HEADS * NA_DH

LANES = 128
V7X_VMEM_LIMIT_BYTES = 56 * 1024 * 1024

TOKEN_BLOCK = 512
GLA_CHUNK = 128
GLA_SAFE_LOG_DECAY = 40.0
NA_ROWS_PER_BLOCK = TOKEN_BLOCK // GRID_W
MOE_ROWS = 512
FINAL_BLOCK = 256
NEG_BIG = -1e30

F32 = jnp.float32
BF16 = jnp.bfloat16


def _cparams(*sem):
    return pltpu.CompilerParams(dimension_semantics=sem, vmem_limit_bytes=V7X_VMEM_LIMIT_BYTES)


def _dot(a, b, precision=None):
    return jnp.dot(a, b, preferred_element_type=F32, precision=precision)


def _dot_nt(a, b, precision=None):
    return lax.dot_general(a, b, (((1,), (1,)), ((), ())), preferred_element_type=F32, precision=precision)


def _dot_tn(a, b):
    return lax.dot_general(a, b, (((0,), (0,)), ((), ())), preferred_element_type=F32)


def _layer_norm(x, g, b):
    mu = jnp.mean(x, axis=-1, keepdims=True)
    xc = x - mu
    var = jnp.mean(xc * xc, axis=-1, keepdims=True)
    return xc * lax.rsqrt(var + LN_EPS) * g + b


def _slab_load(ref, n, width):
    per = width // LANES
    return jnp.concatenate([ref[pl.ds(s, n, stride=per), :] for s in range(per)], axis=-1)


def _slab_store(ref, x):
    n, width = x.shape
    per = width // LANES
    for s in range(per):
        ref[pl.ds(s, n, stride=per), :] = x[:, s * LANES:(s + 1) * LANES]


def _seq_local(blk, groups):
    loc = lax.rem(blk - groups[-1][0], groups[-1][1])
    bps = jnp.int32(groups[-1][1])
    for (first, per), nxt in zip(reversed(groups[:-1]), reversed(groups[1:])):
        inside = blk < nxt[0]
        loc = jnp.where(inside, lax.rem(blk - first, per), loc)
        bps = jnp.where(inside, per, bps)
    return loc, bps


def _part_specs(parts, block_rows, lead=None):
    specs, firsts, first = [], [], 0
    for a in parts:
        rows, width = a.shape[-2], a.shape[-1]
        nb = rows // block_rows
        if lead is None:
            spec = pl.BlockSpec((block_rows, width), lambda i, first=first, nb=nb: (jnp.clip(i - first, 0, nb - 1), 0))
        else:
            spec = pl.BlockSpec((None, block_rows, width),
                                lambda i, first=first, nb=nb: (lead, jnp.clip(i - first, 0, nb - 1), 0))
        specs.append(spec)
        firsts.append(first)
        first += nb
    return specs, tuple(firsts)


def _select_part(i, refs, firsts):
    x = refs[0][...]
    for ref, first in zip(refs[1:], firsts[1:]):
        x = jnp.where(i >= first, ref[...], x)
    return x


def _inproj_kernel(*refs, firsts, apply_ln):
    x_refs, refs = refs[:len(firsts)], refs[len(firsts):]
    g_ref, b_ref, wg_ref, wlr_ref, wgk_ref, bgk_ref, wn_ref = refs[:7]
    out_refs = refs[7:]
    if apply_ln:
        x0_ref, qk_ref, v_ref, vt_ref, gg_ref, lg_ref, qn_ref, kn_ref, vn_ref = out_refs
    else:
        qk_ref, v_ref, vt_ref, gg_ref, lg_ref, qn_ref, kn_ref, vn_ref = out_refs
    x = _select_part(pl.program_id(0), x_refs, firsts)
    if apply_ln:
        x = _layer_norm(x, g_ref[...], b_ref[...])
        x0_ref[...] = x
    xb = x.astype(BF16)
    zg = _dot(xb, wg_ref[...])
    qk_ref[:, :GLA_QK] = (zg[:, :GLA_QK] * GLA_DK ** -0.5).astype(BF16)
    qk_ref[:, GLA_QK:] = zg[:, GLA_QK:2 * GLA_QK].astype(BF16)
    v = zg[:, 2 * GLA_QK:2 * GLA_QK + GLA_V]
    v_ref[...] = v.astype(BF16)
    vt_ref[...] = v.T.astype(BF16)
    gg_ref[...] = zg[:, 2 * GLA_QK + GLA_V:]
    lr = _dot(xb, wlr_ref[...])
    pre = _dot(lr.astype(BF16), wgk_ref[...]) + bgk_ref[...]
    log_sig = jnp.minimum(pre, 0.0) - jnp.log1p(jnp.exp(-jnp.abs(pre)))
    lg_ref[...] = log_sig * (1.0 / GLA_NORMALIZER)
    zn = _dot(xb, wn_ref[...])
    qn_ref[...] = (zn[:, :NA_W] * NA_DH ** -0.5).astype(BF16)
    kn_ref[...] = zn[:, NA_W:2 * NA_W].astype(BF16)
    vn_ref[...] = zn[:, 2 * NA_W:].astype(BF16)


def _inproj(x_parts, ln_g, ln_b, wg, wlr, wgk, bgk, wn, *, apply_ln):
    n = sum(a.shape[0] for a in x_parts)
    d = x_parts[0].shape[1]
    tm = TOKEN_BLOCK
    row = lambda w: pl.BlockSpec((tm, w), lambda i: (i, 0))
    full = lambda a: pl.BlockSpec(a.shape, lambda i: (0,) * a.ndim)
    x_specs, firsts = _part_specs(x_parts, tm)
    outs = [(GLA_QK * 2, BF16, False), (GLA_V, BF16, False), (GLA_V, BF16, True), (GLA_V, F32, False),
            (2 * GLA_QK, F32, False), (NA_W, BF16, False), (NA_W, BF16, False), (NA_W, BF16, False)]
    if apply_ln:
        outs = [(d, F32, False)] + outs
    return pl.pallas_call(
        functools.partial(_inproj_kernel, firsts=firsts, apply_ln=apply_ln),
        grid=(n // tm,),
        in_specs=x_specs + [full(ln_g), full(ln_b), full(wg), full(wlr), full(wgk), full(bgk), full(wn)],
        out_specs=[pl.BlockSpec((w, tm), lambda i: (0, i)) if tr else row(w) for w, _, tr in outs],
        out_shape=[jax.ShapeDtypeStruct((w, n) if tr else (n, w), t) for w, t, tr in outs],
        compiler_params=_cparams("arbitrary"),
        name="inproj_ln" if apply_ln else "inproj",
    )(*x_parts, ln_g, ln_b, wg, wlr, wgk, bgk, wn)


GLA_HEADS_PER_TILE = LANES // GLA_DK
assert GLA_DV == LANES and GLA_HEADS % GLA_HEADS_PER_TILE == 0


def _head_lane_masks(rows):
    lane = lax.broadcasted_iota(jnp.int32, (rows, LANES), 1)
    return [(lane >= hh * GLA_DK) & (lane < (hh + 1) * GLA_DK) for hh in range(GLA_HEADS_PER_TILE)]


def _gla_chunk_fast(qk_ref, v_ref, vt_ref, lg_ref, o_ref, s_ref, d, c, fwd):
    C = GLA_CHUNK
    rows = pl.ds(c * C, C)
    r_i = lax.broadcasted_iota(jnp.int32, (C, C), 0)
    c_i = lax.broadcasted_iota(jnp.int32, (C, C), 1)
    tri = ((r_i >= c_i) if fwd else (r_i <= c_i)).astype(BF16)
    lg = lg_ref[rows, :]
    lg1 = lg.astype(BF16)
    rem = lg - lg1.astype(F32)
    lg2 = rem.astype(BF16)
    lg3 = (rem - lg2.astype(F32)).astype(BF16)
    cum = _dot(tri, lg1) + (_dot(tri, lg2) + _dot(tri, lg3))
    yield
    tot = cum[C - 1:C, :] if fwd else cum[0:1, :]
    q = qk_ref[rows, :GLA_QK].astype(F32)
    k = qk_ref[rows, GLA_QK:].astype(F32)
    qt = (q * jnp.exp(cum)).astype(BF16)
    kt = (k * jnp.exp(-cum)).astype(BF16)
    kd = (k * jnp.exp(tot - cum)).astype(BF16)
    dec = jnp.exp(tot)
    mask = (r_i >= c_i) if fwd else (r_i < c_i)
    in_head = _head_lane_masks(C)
    zero = jnp.zeros((C, LANES), BF16)
    tiles = [slice(p * LANES, (p + 1) * LANES) for p in range(GLA_HEADS // GLA_HEADS_PER_TILE)]
    yield
    states, aos = [], []
    for p, tile in enumerate(tiles):
        s = s_ref[d, p]
        q_heads = jnp.concatenate([jnp.where(m, qt[:, tile], zero) for m in in_head], axis=0)
        aos.append(_dot_nt(q_heads, jnp.concatenate([kt[:, tile], s.astype(BF16)], axis=0)))
        states.append(s)
        yield
    for p, tile in enumerate(tiles):
        vt = [vt_ref[(p * GLA_HEADS_PER_TILE + hh) * GLA_DV:(p * GLA_HEADS_PER_TILE + hh + 1) * GLA_DV,
                     c * C:(c + 1) * C] for hh in range(GLA_HEADS_PER_TILE)]
        kd_heads = jnp.concatenate([jnp.where(m, kd[:, tile], zero) for m in in_head], axis=0)
        s_ref[d, p] = dec[:, tile] * states[p] + _dot(jnp.concatenate(vt, axis=1), kd_heads)
        yield
    for p in range(len(tiles)):
        for hh in range(GLA_HEADS_PER_TILE):
            h = p * GLA_HEADS_PER_TILE + hh
            vs = slice(h * GLA_DV, (h + 1) * GLA_DV)
            a = jnp.where(mask, aos[p][hh * C:(hh + 1) * C, :C], 0.0).astype(BF16)
            o_ref[rows, vs] = _dot(a, v_ref[rows, vs]) + aos[p][hh * C:(hh + 1) * C, C:]
            yield


GLA_STAGES_TO_STATE = 2 + 2 * (GLA_HEADS // GLA_HEADS_PER_TILE)


def _interleave(staggered):
    pending = sorted(staggered, key=lambda item: item[0])
    live, tick = [], 0
    while pending or live:
        while pending and pending[0][0] <= tick:
            live.append(pending.pop(0)[1])
        for g in list(live):
            try:
                next(g)
            except StopIteration:
                live.remove(g)
        tick += 1


def _gla_block_slow(qk_ref, v_ref, lg_ref, o_ref, s_ref, d, fwd, tb):
    G = 16

    def group(n, carry):
        rows = pl.ds(pl.multiple_of((n if fwd else tb // G - 1 - n) * G, G), G)
        gate = jnp.exp(lg_ref[rows, :])
        q = qk_ref[rows, :GLA_QK].astype(F32)
        k = qk_ref[rows, GLA_QK:]
        qs = (q if fwd else q * gate).astype(BF16)
        row = lax.broadcasted_iota(jnp.int32, (G, LANES), 0)
        in_head = _head_lane_masks(G)
        zero = jnp.zeros((G, LANES), BF16)
        for p in range(GLA_HEADS // GLA_HEADS_PER_TILE):
            tile = slice(p * LANES, (p + 1) * LANES)
            heads = [p * GLA_HEADS_PER_TILE + hh for hh in range(GLA_HEADS_PER_TILE)]
            vs = [slice(h * GLA_DV, (h + 1) * GLA_DV) for h in heads]
            v = [v_ref[rows, sl] for sl in vs]
            k_heads = [jnp.where(m, k[:, tile], zero) for m in in_head]
            s = s_ref[d, p]
            o = [jnp.zeros((G, GLA_DV), F32) for _ in heads]
            for r in (range(G) if fwd else reversed(range(G))):
                upd = None
                for hh in range(len(heads)):
                    term = _dot_tn(jnp.where(row == r, v[hh], zero), k_heads[hh])
                    upd = term if upd is None else upd + term
                s_new = gate[r:r + 1, tile] * s + upd
                s_read = (s_new if fwd else s).astype(BF16)
                for hh in range(len(heads)):
                    o[hh] = o[hh] + _dot_nt(jnp.where((row == r) & in_head[hh], qs[:, tile], zero), s_read)
                s = s_new
            for hh in range(len(heads)):
                o_ref[rows, vs[hh]] = o[hh]
            s_ref[d, p] = s
        return carry

    lax.fori_loop(0, tb // G, group, 0)


def _gla_kernel(qkf_ref, vf_ref, vtf_ref, lgf_ref, qkb_ref, vb_ref, vtb_ref, lgb_ref, of_ref, ob_ref, s_ref,
                *, groups, tb):
    i = pl.program_id(0)
    nblk = pl.num_programs(0)
    loc_f, _ = _seq_local(i, groups)
    loc_b, bps_b = _seq_local(nblk - 1 - i, groups)

    @pl.when(loc_f == 0)
    def _():
        s_ref[0] = jnp.zeros(s_ref.shape[1:], F32)

    @pl.when(loc_b == bps_b - 1)
    def _():
        s_ref[1] = jnp.zeros(s_ref.shape[1:], F32)

    nc = tb // GLA_CHUNK
    worst = jnp.float32(0.0)
    for ref in (lgf_ref, lgb_ref):
        for c in range(nc):
            tot = jnp.sum(ref[pl.ds(c * GLA_CHUNK, GLA_CHUNK), :], axis=0, keepdims=True)
            worst = jnp.minimum(worst, jnp.min(tot))
    safe = worst > -GLA_SAFE_LOG_DECAY

    @pl.when(safe)
    def _():
        work = []
        for c in range(nc):
            work.append((c * GLA_STAGES_TO_STATE,
                         _gla_chunk_fast(qkf_ref, vf_ref, vtf_ref, lgf_ref, of_ref, s_ref, 0, c, True)))
            work.append((c * GLA_STAGES_TO_STATE,
                         _gla_chunk_fast(qkb_ref, vb_ref, vtb_ref, lgb_ref, ob_ref, s_ref, 1, nc - 1 - c, False)))
        _interleave(work)

    @pl.when(jnp.logical_not(safe))
    def _():
        _gla_block_slow(qkf_ref, vf_ref, lgf_ref, of_ref, s_ref, 0, True, tb)
        _gla_block_slow(qkb_ref, vb_ref, lgb_ref, ob_ref, s_ref, 1, False, tb)


def _gla(qk, v, vt, lg, groups):
    n = qk.shape[0]
    tb = TOKEN_BLOCK
    nblk = n // tb
    fwd = lambda w, j: pl.BlockSpec((tb, w), lambda i: (i, j))
    bwd = lambda w, j: pl.BlockSpec((tb, w), lambda i: (nblk - 1 - i, j))
    return pl.pallas_call(
        functools.partial(_gla_kernel, groups=groups, tb=tb),
        grid=(nblk,),
        in_specs=[fwd(2 * GLA_QK, 0), fwd(GLA_V, 0), pl.BlockSpec((GLA_V, tb), lambda i: (0, i)), fwd(GLA_QK, 0),
                  bwd(2 * GLA_QK, 0), bwd(GLA_V, 0), pl.BlockSpec((GLA_V, tb), lambda i: (0, nblk - 1 - i)),
                  bwd(GLA_QK, 1)],
        out_specs=[fwd(GLA_V, 0), bwd(GLA_V, 0)],
        out_shape=[jax.ShapeDtypeStruct((n, GLA_V), F32)] * 2,
        scratch_shapes=[pltpu.VMEM((2, GLA_HEADS // GLA_HEADS_PER_TILE, GLA_DV, LANES), F32)],
        compiler_params=_cparams("arbitrary"),
        name="gla",
    )(qk, v, vt, lg, qk, v, vt, lg)


def _na_bias_table(rpb):
    w = GRID_W
    col = np.arange(w)
    start = np.clip(col - NA_KW // 2, 0, w - NA_KW)
    kc = np.arange(w)
    in_win = (kc[None, :] >= start[:, None]) & (kc[None, :] < start[:, None] + NA_KW)
    col_off = np.clip(kc[None, :] - col[:, None] + (NA_KW - 1), 0, 2 * NA_KW - 2)
    shift = np.arange(NA_KH)
    row_off = np.arange(NA_KH)[None, :] - shift[:, None] + (NA_KH - 1)
    t = rpb[:, row_off]
    t = t[:, :, :, col_off]
    t = jnp.where(jnp.asarray(in_win)[None, None, None], t, NEG_BIG)
    t = t.transpose(1, 0, 3, 2, 4)
    heads = LANES // NA_DH
    return t.reshape(NA_KH, NA_HEADS // heads, heads * w, NA_KH * w).astype(F32)


def _na_kernel(q_ref, kp_ref, kc_ref, kn_ref, vp_ref, vc_ref, vn_ref, bias_ref, o_ref, kbuf, vbuf, *, groups):
    tb = TOKEN_BLOCK
    w = GRID_W
    rpb_rows = NA_ROWS_PER_BLOCK
    blk = pl.program_id(1)
    loc, bps = _seq_local(blk, groups)
    rows_in_seq = bps * rpb_rows
    for j, (kr, vr) in enumerate(((kp_ref, vp_ref), (kc_ref, vc_ref), (kn_ref, vn_ref))):
        kbuf[pl.ds(j * tb, tb), :] = kr[...]
        vbuf[pl.ds(j * tb, tb), :] = vr[...]
    lane = lax.broadcasted_iota(jnp.int32, (w, LANES), 1)
    heads = LANES // NA_DH
    in_head = [(lane >= hh * NA_DH) & (lane < (hh + 1) * NA_DH) for hh in range(heads)]
    win = NA_KH * w

    def probs(j):
        r = loc * rpb_rows + j
        rs = jnp.clip(r - NA_KH // 2, 0, rows_in_seq - NA_KH)
        off = pl.multiple_of((rs - loc * rpb_rows + rpb_rows) * w, w)
        q = q_ref[pl.ds(j * w, w), :]
        q_heads = jnp.concatenate([jnp.where(m, q, jnp.zeros_like(q)) for m in in_head], axis=0)
        s = _dot_nt(q_heads, kbuf[pl.ds(off, win), :]) + bias_ref[r - rs, 0]
        p = jnp.exp(s - jnp.max(s, axis=-1, keepdims=True))
        return j, off, p.astype(BF16), jnp.sum(p, axis=-1, keepdims=True)

    def output(j, off, p, l):
        o = _dot(p, vbuf[pl.ds(off, win), :]) * (1.0 / l)
        out = o[:w]
        for hh in range(1, heads):
            out = jnp.where(in_head[hh], o[hh * w:(hh + 1) * w], out)
        o_ref[pl.ds(j * w, w), :] = out.astype(o_ref.dtype)

    pending = []
    for j in range(rpb_rows):
        pending.append(probs(j))
        if len(pending) > 2:
            output(*pending.pop(0))
    for item in pending:
        output(*item)


def _na(q, k, v, bias, groups):
    n = q.shape[0]
    tb = TOKEN_BLOCK
    nblk = n // tb
    nhp = NA_W // LANES
    cur = pl.BlockSpec((tb, LANES), lambda hp, i: (i, hp))
    prv = pl.BlockSpec((tb, LANES), lambda hp, i: (jnp.maximum(i - 1, 0), hp))
    nxt = pl.BlockSpec((tb, LANES), lambda hp, i: (jnp.minimum(i + 1, nblk - 1), hp))
    return pl.pallas_call(
        functools.partial(_na_kernel, groups=groups),
        grid=(nhp, nblk),
        in_specs=[cur, prv, cur, nxt, prv, cur, nxt,
                  pl.BlockSpec((NA_KH, 1) + bias.shape[2:], lambda hp, i: (0, hp, 0, 0))],
        out_specs=cur,
        out_shape=jax.ShapeDtypeStruct((n, NA_W), BF16),
        scratch_shapes=[pltpu.VMEM((3 * tb, LANES), BF16), pltpu.VMEM((3 * tb, LANES), BF16)],
        compiler_params=_cparams("parallel", "parallel"),
        name="natten",
    )(q, k, k, k, v, v, v, bias)


def _outproj_kernel(of_ref, ob_ref, gg_ref, ona_ref, x_ref, ng_ref, wog_ref, won_ref, g1_ref, b1_ref,
                    wrh_ref, wrl_ref, br_ref, x1_ref, e_ref, p_ref, pos_ref, cnt_ref, run_ref, ut_ref,
                    *, alpha):
    i = pl.program_id(0)
    tm = x_ref.shape[0]

    @pl.when(i == 0)
    def _():
        run_ref[...] = jnp.zeros(run_ref.shape, F32)
        r_i = lax.broadcasted_iota(jnp.int32, (tm, tm), 0)
        c_i = lax.broadcasted_iota(jnp.int32, (tm, tm), 1)
        ut_ref[...] = (r_i < c_i).astype(BF16)

    o = of_ref[...] + ob_ref[...]
    parts = []
    for h in range(GLA_HEADS):
        oh = o[:, h * GLA_DV:(h + 1) * GLA_DV]
        parts.append(oh * lax.rsqrt(jnp.mean(oh * oh, axis=-1, keepdims=True) + RMS_EPS))
    g = gg_ref[...]
    o = jnp.concatenate(parts, axis=-1) * ng_ref[...] * (g * jax.nn.sigmoid(g))
    mix = _dot(o.astype(BF16), wog_ref[...]) + _dot(ona_ref[...], won_ref[...])
    x1 = _layer_norm(alpha * x_ref[...] + mix, g1_ref[...], b1_ref[...])
    _slab_store(x1_ref, x1)

    x_hi = x1.astype(BF16)
    x_lo = (x1 - x_hi.astype(F32)).astype(BF16)
    logits = (_dot(x_hi, wrh_ref[...]) + (_dot(x_hi, wrl_ref[...]) + _dot(x_lo, wrh_ref[...])) + br_ref[...])
    lg = logits.T[:N_EXPERTS]
    eid = lax.broadcasted_iota(jnp.int32, (N_EXPERTS, tm), 0)
    out_row = lax.broadcasted_iota(jnp.int32, (8, tm), 0)
    chosen = jnp.zeros((N_EXPERTS, tm), F32)
    e_out = jnp.zeros((8, tm), jnp.int32)
    vals, sels = [], []
    for k in range(TOP_K):
        m = jnp.max(lg, axis=0, keepdims=True)
        idx = jnp.min(jnp.where(lg == m, eid, N_EXPERTS), axis=0, keepdims=True)
        sel = eid == idx
        sels.append(sel)
        vals.append(m)
        e_out = jnp.where(out_row == k, idx, e_out)
        chosen = jnp.where(sel, 1.0, chosen)
        lg = jnp.where(sel, -jnp.inf, lg)
    ex = [jnp.exp(v - vals[0]) for v in vals]
    inv = 1.0 / functools.reduce(lambda a, b: a + b, ex)
    p_out = jnp.zeros((8, tm), F32)
    for k in range(TOP_K):
        p_out = jnp.where(out_row == k, ex[k] * inv, p_out)
    p_ref[...] = p_out
    e_ref[...] = e_out
    run = run_ref[...]
    before = _dot(chosen.astype(BF16), ut_ref[...]) + run[:, :1]
    pos = jnp.zeros((8, tm), jnp.int32)
    for k in range(TOP_K):
        pk = jnp.sum(jnp.where(sels[k], before, 0.0), axis=0, keepdims=True)
        pos = jnp.where(out_row == k, pk.astype(jnp.int32), pos)
    pos_ref[...] = pos
    run = run + jnp.sum(chosen, axis=1, keepdims=True)
    run_ref[...] = run
    cnt_ref[...] = run.astype(jnp.int32)


def _outproj(o_f, o_b, gg, o_na, x, ng, wog, won, g1, b1, wrh, wrl, br, *, alpha):
    n, d = x.shape
    tm = TOKEN_BLOCK
    row = lambda w: pl.BlockSpec((tm, w), lambda i: (i, 0))
    col = pl.BlockSpec((8, tm), lambda i: (0, i))
    full = lambda a: pl.BlockSpec(a.shape, lambda i: (0,) * a.ndim)
    return pl.pallas_call(
        functools.partial(_outproj_kernel, alpha=alpha),
        grid=(n // tm,),
        in_specs=[row(GLA_V), row(GLA_V), row(GLA_V), row(NA_W), row(d), full(ng), full(wog), full(won),
                  full(g1), full(b1), full(wrh), full(wrl), full(br)],
        out_specs=[pl.BlockSpec((tm * d // LANES, LANES), lambda i: (i, 0)), col, col, col,
                   pl.BlockSpec((N_EXPERTS, LANES), lambda i: (0, 0))],
        out_shape=[jax.ShapeDtypeStruct((n * d // LANES, LANES), F32),
                   jax.ShapeDtypeStruct((8, n), jnp.int32), jax.ShapeDtypeStruct((8, n), F32),
                   jax.ShapeDtypeStruct((8, n), jnp.int32), jax.ShapeDtypeStruct((N_EXPERTS, LANES), jnp.int32)],
        scratch_shapes=[pltpu.VMEM((N_EXPERTS, LANES), F32), pltpu.VMEM((tm, tm), BF16)],
        compiler_params=_cparams("arbitrary"),
        name="outproj_router",
    )(o_f, o_b, gg, o_na, x, ng, wog, won, g1, b1, wrh, wrl, br)


def _slab_rows(ref, row, per):
    return ref.at[pl.ds(pl.multiple_of(row * per, per), per)]


DMA_ISSUE_UNROLL = 8


def _dispatch_kernel(dest_ref, x_ref, xs_ref, sem, *, per):
    tm = x_ref.shape[0] // per

    def group(g, carry):
        for u in range(DMA_ISSUE_UNROLL):
            t = g * DMA_ISSUE_UNROLL + u
            for k in range(TOP_K):
                pltpu.make_async_copy(_slab_rows(x_ref, t, per), _slab_rows(xs_ref, dest_ref[t * TOP_K + k], per),
                                      sem).start(priority=k % 2)
        return carry

    lax.fori_loop(0, tm // DMA_ISSUE_UNROLL, group, 0)
    for k in range(TOP_K):
        pltpu.make_async_copy(x_ref, xs_ref.at[pl.ds(0, tm * per)], sem).wait()


def _dispatch(dest_flat, x1_slab, n_slots, per):
    n = x1_slab.shape[0] // per
    tm = TOKEN_BLOCK
    return pl.pallas_call(
        functools.partial(_dispatch_kernel, per=per),
        grid=(n // tm,),
        in_specs=[pl.BlockSpec((tm * TOP_K,), lambda i: (i,), memory_space=pltpu.SMEM),
                  pl.BlockSpec((tm * per, LANES), lambda i: (i, 0))],
        out_specs=pl.BlockSpec(memory_space=pl.ANY),
        out_shape=jax.ShapeDtypeStruct((n_slots * per, LANES), x1_slab.dtype),
        scratch_shapes=[pltpu.SemaphoreType.DMA(())],
        compiler_params=_cparams("arbitrary"),
        name="moe_dispatch",
    )(dest_flat, x1_slab)


def _experts_kernel(be_ref, nact_ref, xs_ref, wgu_ref, bgu_ref, wd_ref, bd_ref, y_ref, wgu_bf, wd_bf):
    i = pl.program_id(0)
    d, de = wd_ref.shape[2], wd_ref.shape[1]
    c = MOE_ROWS

    @pl.when((i == 0) | (be_ref[i] != be_ref[jnp.maximum(i - 1, 0)]))
    def _():
        wgu_bf[...] = wgu_ref[0].astype(BF16)
        wd_bf[...] = wd_ref[0].astype(BF16)

    @pl.when(i < nact_ref[0])
    def _():
        x = _slab_load(xs_ref, c, d).astype(BF16)
        h = _dot(x, wgu_bf[...]) + bgu_ref[0]
        gate = jnp.minimum(h[:, :de], SWIGLU_LIMIT)
        up = jnp.clip(h[:, de:], -SWIGLU_LIMIT, SWIGLU_LIMIT)
        glu = gate * jax.nn.sigmoid(gate * SWIGLU_ALPHA)
        act = ((up + 1.0) * glu).astype(BF16)
        _slab_store(y_ref, _dot(act, wd_bf[...]) + bd_ref[0])

    @pl.when(i >= nact_ref[0])
    def _():
        y_ref[...] = jnp.zeros(y_ref.shape, y_ref.dtype)


def _experts(blk_e, nact, xs, layer, wgu, bgu, wd, bd):
    d, de = wd.shape[3], wd.shape[2]
    per = d // LANES
    c = MOE_ROWS
    slab = pl.BlockSpec((c * per, LANES), lambda i, be, na: (i, 0))
    grid_spec = pltpu.PrefetchScalarGridSpec(
        num_scalar_prefetch=2,
        grid=(xs.shape[0] // (c * per),),
        in_specs=[slab,
                  pl.BlockSpec((None, 1, d, 2 * de), lambda i, be, na: (layer, be[i], 0, 0)),
                  pl.BlockSpec((1, 1, 2 * de), lambda i, be, na: (be[i], 0, 0)),
                  pl.BlockSpec((None, 1, de, d), lambda i, be, na: (layer, be[i], 0, 0)),
                  pl.BlockSpec((1, 1, d), lambda i, be, na: (be[i], 0, 0))],
        out_specs=slab,
        scratch_shapes=[pltpu.VMEM((d, 2 * de), BF16), pltpu.VMEM((de, d), BF16)],
    )
    return pl.pallas_call(
        _experts_kernel,
        grid_spec=grid_spec,
        out_shape=jax.ShapeDtypeStruct(xs.shape, F32),
        compiler_params=_cparams("arbitrary"),
        name="moe_experts",
    )(blk_e, nact, xs, wgu, bgu, wd, bd)


def _final_kernel(*refs, ple_firsts, out_firsts, alpha):
    dest_ref, dest_next_ref, x1_ref, prob_ref = refs[:4]
    refs = refs[4:]
    ple_refs, refs = refs[:len(ple_firsts)], refs[len(ple_firsts):]
    y_ref, wpp_ref, wpg_ref, bpg_ref, g2_ref, b2_ref = refs[:6]
    refs = refs[6:]
    out_refs, scratch = refs[:len(out_firsts)], refs[len(out_firsts):]
    ybufs, sems = scratch[:TOP_K], scratch[TOP_K]
    tm, d = out_refs[0].shape
    per = d // LANES
    i = pl.program_id(0)
    slot = lax.rem(i, 2)

    def gather(idx_ref, into):
        def group(g, carry):
            for u in range(DMA_ISSUE_UNROLL):
                t = g * DMA_ISSUE_UNROLL + u
                for k in range(TOP_K):
                    pltpu.make_async_copy(_slab_rows(y_ref, idx_ref[t * TOP_K + k], per),
                                          _slab_rows(ybufs[k], into * tm + t, per),
                                          sems.at[into]).start(priority=k % 2)
            return carry
        lax.fori_loop(0, tm // DMA_ISSUE_UNROLL, group, 0)

    @pl.when(i == 0)
    def _():
        gather(dest_ref, slot)

    @pl.when(i + 1 < pl.num_programs(0))
    def _():
        gather(dest_next_ref, 1 - slot)

    half = pl.ds(pl.multiple_of(slot * (tm * per), tm * per), tm * per)
    for k in range(TOP_K):
        pltpu.make_async_copy(y_ref.at[pl.ds(0, tm * per)], ybufs[k].at[half], sems.at[slot]).wait()
    prob = prob_ref[...]
    moe = None
    for k in range(TOP_K):
        yk = _slab_load(ybufs[k].at[half], tm, d)
        moe = prob[:, k:k + 1] * yk if moe is None else moe + prob[:, k:k + 1] * yk
    r = alpha * _slab_load(x1_ref, tm, d) + moe
    gate = jax.nn.sigmoid(_dot(r.astype(BF16), wpg_ref[...]) + bpg_ref[...])
    ple = _select_part(i, ple_refs, ple_firsts)
    u = _dot(ple.astype(BF16), wpp_ref[...]) * gate
    res = _layer_norm(r + u, g2_ref[...], b2_ref[...])
    if len(out_refs) == 1:
        out_refs[0][...] = res
    else:
        bounds = list(out_firsts[1:]) + [pl.num_programs(0)]
        for ref, first, end in zip(out_refs, out_firsts, bounds):
            @pl.when((i >= first) & (i < end))
            def _(ref=ref):
                ref[...] = res


def _final(dest_flat, x1_slab, prob, ple_parts, layer, y_slab, wpp, wpg, bpg, g2, b2, out_rows, *, alpha):
    n = prob.shape[0]
    d = wpg.shape[0]
    per = d // LANES
    tm = FINAL_BLOCK
    nblk = n // tm
    row = lambda w: pl.BlockSpec((tm, w), lambda i: (i, 0))
    full = lambda a: pl.BlockSpec(a.shape, lambda i: (0,) * a.ndim)
    ple_specs, ple_firsts = _part_specs(ple_parts, tm, lead=layer)
    out_shapes = [jax.ShapeDtypeStruct((r, d), F32) for r in out_rows]
    out_specs, out_firsts = _part_specs(out_shapes, tm)
    return pl.pallas_call(
        functools.partial(_final_kernel, ple_firsts=ple_firsts, out_firsts=out_firsts, alpha=alpha),
        grid=(nblk,),
        in_specs=[pl.BlockSpec((tm * TOP_K,), lambda i: (i,), memory_space=pltpu.SMEM),
                  pl.BlockSpec((tm * TOP_K,), lambda i: (jnp.minimum(i + 1, nblk - 1),), memory_space=pltpu.SMEM),
                  pl.BlockSpec((tm * per, LANES), lambda i: (i, 0)), row(prob.shape[1])]
                 + ple_specs
                 + [pl.BlockSpec(memory_space=pl.ANY), full(wpp), full(wpg), full(bpg), full(g2), full(b2)],
        out_specs=out_specs,
        out_shape=out_shapes,
        scratch_shapes=[pltpu.VMEM((2 * tm * per, LANES), F32)] * TOP_K + [pltpu.SemaphoreType.DMA((2,))],
        compiler_params=_cparams("arbitrary"),
        name="moe_combine_final",
    )(dest_flat, dest_flat, x1_slab, prob, *ple_parts, y_slab, wpp, wpg, bpg, g2, b2)


def _routing_tables(top_e, pos, counts, n_blocks):
    c = MOE_ROWS
    e = jnp.arange(N_EXPERTS, dtype=jnp.int32)
    padded = (counts + c - 1) // c * c
    pad_end = jnp.sum(jnp.where(e[:, None] <= e[None, :], padded[:, None], 0), axis=0)
    pad_start = pad_end - padded
    start_of = jnp.sum(jnp.where(top_e[None] == e[:, None, None], pad_start[:, None, None], 0), axis=0)
    dest = (start_of + pos).T.reshape(-1)
    blk_first = jnp.arange(n_blocks, dtype=jnp.int32) * c
    blk_e = jnp.minimum(jnp.sum((pad_end[None, :] <= blk_first[:, None]).astype(jnp.int32), axis=1),
                        N_EXPERTS - 1)
    nact = pad_end[-1:] // c
    return dest.astype(jnp.int32), blk_e.astype(jnp.int32), nact.astype(jnp.int32)


def _layer(x_parts, ple_parts, lw, groups, out_rows, *, alpha, ln0):
    d = x_parts[0].shape[1]
    outs = _inproj(x_parts, ln0[0], ln0[1], lw['wg'], lw['wlr'], lw['wgk'], lw['bgk'], lw['wn'], apply_ln=ln0[2])
    if ln0[2]:
        x, outs = outs[0], outs[1:]
    else:
        (x,) = x_parts
    n = x.shape[0]
    qk, v, vt, gg, lg, qn, kn, vn = outs
    o_f, o_b = _gla(qk, v, vt, lg, groups)
    o_na = _na(qn, kn, vn, lw['na_bias'], groups)
    x1, top_e, prob, pos, cnt = _outproj(o_f, o_b, gg, o_na, x, lw['ng'], lw['wog'], lw['won'],
                                         lw['g1'], lw['b1'], lw['wrh'], lw['wrl'], lw['br'], alpha=alpha)
    n_blocks = (n * TOP_K + N_EXPERTS * (MOE_ROWS - 1) + MOE_ROWS - 1) // MOE_ROWS
    dest, blk_e, nact = _routing_tables(top_e[:TOP_K], pos[:TOP_K], cnt[:, 0], n_blocks)
    xs = _dispatch(dest, x1, n_blocks * MOE_ROWS, d // LANES)
    y = _experts(blk_e, nact, xs, lw['layer'], lw['wgu'], lw['bgu'], lw['wd'], lw['bd'])
    return _final(dest, x1, prob.T, ple_parts, lw['layer'], y, lw['wpp'], lw['wpg'], lw['bpg'], lw['g2'], lw['b2'],
                  out_rows, alpha=alpha)


def kernel(x_prompt, x_sample, p_prompt, p_sample, emb_ln_g, emb_ln_b, w_in, w_gk_f, b_gk_f, w_gk_b, b_gk_b, gla_norm_g, rpb, w_out, ln1_g, ln1_b, w_router, b_router, w_gu, b_gu, w_down, b_down, w_ple_proj, w_ple_gate, b_ple_gate, ln2_g, ln2_b):
    depth, d = w_in.shape[0], w_in.shape[1]
    alpha = float((2 * depth) ** 0.25)
    tb = TOKEN_BLOCK
    groups, first = [], 0
    for a in (x_prompt, x_sample):
        b, t, _ = a.shape
        assert t % tb == 0 and t % GRID_W == 0 and t // GRID_W >= NA_KH
        groups.append((first, t // tb))
        first += b * t // tb
    groups = tuple(groups)
    x_parts = [x_prompt.reshape(-1, d), x_sample.reshape(-1, d)]
    ple_parts = [p_prompt.reshape(depth, -1, p_prompt.shape[-1]), p_sample.reshape(depth, -1, p_sample.shape[-1])]
    n_all = sum(a.shape[0] for a in x_parts)
    row = lambda a: a.reshape(1, -1).astype(F32)
    c0, c1, c2 = 2 * GLA_QK + 2 * GLA_V, 2 * GLA_QK + 2 * GLA_V + 2 * GLA_RANK, w_in.shape[2]
    for i in range(depth):
        wgk = jnp.zeros((LANES, 2 * GLA_QK), F32)
        wgk = wgk.at[:GLA_RANK, :GLA_QK].set(w_gk_f[i]).at[GLA_RANK:2 * GLA_RANK, GLA_QK:].set(w_gk_b[i])
        wr = jnp.pad(w_router[i].astype(F32), ((0, 0), (0, LANES - N_EXPERTS)))
        lw = dict(
            wg=w_in[i, :, :c0].astype(BF16),
            wlr=jnp.pad(w_in[i, :, c0:c1], ((0, 0), (0, LANES - 2 * GLA_RANK))).astype(BF16),
            wgk=wgk.astype(BF16),
            bgk=jnp.concatenate([b_gk_f[i], b_gk_b[i]]).reshape(1, -1),
            wn=w_in[i, :, c1:c2].astype(BF16),
            na_bias=_na_bias_table(rpb[i]),
            ng=row(gla_norm_g[i]),
            wog=w_out[i, :GLA_V].astype(BF16), won=w_out[i, GLA_V:].astype(BF16),
            g1=row(ln1_g[i]), b1=row(ln1_b[i]),
            wrh=wr.astype(BF16), wrl=(wr - wr.astype(BF16).astype(F32)).astype(BF16),
            br=jnp.pad(b_router[i].astype(F32), (0, LANES - N_EXPERTS)).reshape(1, -1),
            layer=i, wgu=w_gu, bgu=b_gu[i][:, None, :], wd=w_down, bd=b_down[i][:, None, :],
            wpp=w_ple_proj[i].astype(BF16), wpg=w_ple_gate[i].astype(BF16), bpg=row(b_ple_gate[i]),
            g2=row(ln2_g[i]), b2=row(ln2_b[i]),
        )
        last = i == depth - 1
        out_rows = [a.shape[0] for a in x_parts] if last else [n_all]
        outs = _layer(x_parts if i == 0 else [x], ple_parts, lw, groups, out_rows,
                      alpha=alpha, ln0=(row(emb_ln_g), row(emb_ln_b), i == 0))
        if not last:
            (x,) = outs
    return (outs[0].reshape(x_prompt.shape), outs[1].reshape(x_sample.shape))
```

```python
import functools

import numpy as np
import jax
import jax.numpy as jnp
from jax import lax
from jax.experimental import pallas as pl
from jax.experimental.pallas import tpu as pltpu

GRID_W = 64
GLA_HEADS, GLA_DK, GLA_DV, GLA_RANK = 4, 64, 128, 16
GLA_NORMALIZER = 16.0
NA_HEADS, NA_DH, NA_KH, NA_KW = 8, 64, 8, 16
N_EXPERTS, TOP_K = 32, 4
SWIGLU_ALPHA, SWIGLU_LIMIT = 1.702, 7.0
LN_EPS, RMS_EPS = 1e-5, 1e-6

GLA_QK = GLA_HEADS * GLA_DK
GLA_V = GLA_HEADS * GLA_DV
NA_W = NA_HEADS * NA_DH

LANES = 128
V7X_VMEM_LIMIT_BYTES = 56 * 1024 * 1024

TOKEN_BLOCK = 512
GLA_CHUNK = 128
GLA_SAFE_LOG_DECAY = 40.0
NA_ROWS_PER_BLOCK = TOKEN_BLOCK // GRID_W
MOE_ROWS = 1024
FINAL_BLOCK = 512
NEG_BIG = -1e30

F32 = jnp.float32
BF16 = jnp.bfloat16


def _cparams(*sem):
    return pltpu.CompilerParams(dimension_semantics=sem, vmem_limit_bytes=V7X_VMEM_LIMIT_BYTES)


def _dot(a, b, precision=None):
    return jnp.dot(a, b, preferred_element_type=F32, precision=precision)


def _dot_nt(a, b, precision=None):
    return lax.dot_general(a, b, (((1,), (1,)), ((), ())), preferred_element_type=F32, precision=precision)


def _dot_tn(a, b):
    return lax.dot_general(a, b, (((0,), (0,)), ((), ())), preferred_element_type=F32)


def _layer_norm(x, g, b):
    mu = jnp.mean(x, axis=-1, keepdims=True)
    xc = x - mu
    var = jnp.mean(xc * xc, axis=-1, keepdims=True)
    return xc * lax.rsqrt(var + LN_EPS) * g + b


def _slab_load(ref, n, width):
    per = width // LANES
    return jnp.concatenate([ref[pl.ds(s, n, stride=per), :] for s in range(per)], axis=-1)


def _slab_store(ref, x):
    n, width = x.shape
    per = width // LANES
    for s in range(per):
        ref[pl.ds(s, n, stride=per), :] = x[:, s * LANES:(s + 1) * LANES]


def _seq_local(blk, groups):
    loc = lax.rem(blk - groups[-1][0], groups[-1][1])
    bps = jnp.int32(groups[-1][1])
    for (first, per), nxt in zip(reversed(groups[:-1]), reversed(groups[1:])):
        inside = blk < nxt[0]
        loc = jnp.where(inside, lax.rem(blk - first, per), loc)
        bps = jnp.where(inside, per, bps)
    return loc, bps


def _part_specs(parts, block_rows, lead=None):
    specs, firsts, first = [], [], 0
    for a in parts:
        rows, width = a.shape[-2], a.shape[-1]
        nb = rows // block_rows
        if lead is None:
            spec = pl.BlockSpec((block_rows, width), lambda i, first=first, nb=nb: (jnp.clip(i - first, 0, nb - 1), 0))
        else:
            spec = pl.BlockSpec((None, block_rows, width),
                                lambda i, first=first, nb=nb: (lead, jnp.clip(i - first, 0, nb - 1), 0))
        specs.append(spec)
        firsts.append(first)
        first += nb
    return specs, tuple(firsts)


def _select_part(i, refs, firsts):
    x = refs[0][...]
    for ref, first in zip(refs[1:], firsts[1:]):
        x = jnp.where(i >= first, ref[...], x)
    return x


def _inproj_kernel(*refs, firsts, apply_ln):
    x_refs, refs = refs[:len(firsts)], refs[len(firsts):]
    g_ref, b_ref, wg_ref, wlr_ref, wgk_ref, bgk_ref, wn_ref = refs[:7]
    out_refs = refs[7:]
    if apply_ln:
        x0_ref, qk_ref, v_ref, vt_ref, gg_ref, lg_ref, qn_ref, kn_ref, vn_ref = out_refs
    else:
        qk_ref, v_ref, vt_ref, gg_ref, lg_ref, qn_ref, kn_ref, vn_ref = out_refs
    x = _select_part(pl.program_id(0), x_refs, firsts)
    if apply_ln:
        x = _layer_norm(x, g_ref[...], b_ref[...])
        x0_ref[...] = x
    xb = x.astype(BF16)
    zg = _dot(xb, wg_ref[...])
    qk_ref[:, :GLA_QK] = (zg[:, :GLA_QK] * GLA_DK ** -0.5).astype(BF16)
    qk_ref[:, GLA_QK:] = zg[:, GLA_QK:2 * GLA_QK].astype(BF16)
    v = zg[:, 2 * GLA_QK:2 * GLA_QK + GLA_V]
    v_ref[...] = v.astype(BF16)
    vt_ref[...] = v.T.astype(BF16)
    gg_ref[...] = zg[:, 2 * GLA_QK + GLA_V:]
    lr = _dot(xb, wlr_ref[...])
    pre = _dot(lr.astype(BF16), wgk_ref[...]) + bgk_ref[...]
    log_sig = jnp.minimum(pre, 0.0) - jnp.log1p(jnp.exp(-jnp.abs(pre)))
    lg_ref[...] = log_sig * (1.0 / GLA_NORMALIZER)
    zn = _dot(xb, wn_ref[...])
    qn_ref[...] = (zn[:, :NA_W] * NA_DH ** -0.5).astype(BF16)
    kn_ref[...] = zn[:, NA_W:2 * NA_W].astype(BF16)
    vn_ref[...] = zn[:, 2 * NA_W:].astype(BF16)


def _inproj(x_parts, ln_g, ln_b, wg, wlr, wgk, bgk, wn, *, apply_ln):
    n = sum(a.shape[0] for a in x_parts)
    d = x_parts[0].shape[1]
    tm = TOKEN_BLOCK
    row = lambda w: pl.BlockSpec((tm, w), lambda i: (i, 0))
    full = lambda a: pl.BlockSpec(a.shape, lambda i: (0,) * a.ndim)
    x_specs, firsts = _part_specs(x_parts, tm)
    outs = [(GLA_QK * 2, BF16, False), (GLA_V, BF16, False), (GLA_V, BF16, True), (GLA_V, F32, False),
            (2 * GLA_QK, F32, False), (NA_W, BF16, False), (NA_W, BF16, False), (NA_W, BF16, False)]
    if apply_ln:
        outs = [(d, F32, False)] + outs
    return pl.pallas_call(
        functools.partial(_inproj_kernel, firsts=firsts, apply_ln=apply_ln),
        grid=(n // tm,),
        in_specs=x_specs + [full(ln_g), full(ln_b), full(wg), full(wlr), full(wgk), full(bgk), full(wn)],
        out_specs=[pl.BlockSpec((w, tm), lambda i: (0, i)) if tr else row(w) for w, _, tr in outs],
        out_shape=[jax.ShapeDtypeStruct((w, n) if tr else (n, w), t) for w, t, tr in outs],
        compiler_params=_cparams("arbitrary"),
        name="inproj_ln" if apply_ln else "inproj",
    )(*x_parts, ln_g, ln_b, wg, wlr, wgk, bgk, wn)


GLA_HEADS_PER_TILE = LANES // GLA_DK
assert GLA_DV == LANES and GLA_HEADS % GLA_HEADS_PER_TILE == 0


def _head_lane_masks(rows):
    lane = lax.broadcasted_iota(jnp.int32, (rows, LANES), 1)
    return [(lane >= hh * GLA_DK) & (lane < (hh + 1) * GLA_DK) for hh in range(GLA_HEADS_PER_TILE)]


def _gla_chunk_fast(qk_ref, v_ref, vt_ref, lg_ref, o_ref, s_ref, d, c, fwd):
    C = GLA_CHUNK
    rows = pl.ds(c * C, C)
    r_i = lax.broadcasted_iota(jnp.int32, (C, C), 0)
    c_i = lax.broadcasted_iota(jnp.int32, (C, C), 1)
    tri = ((r_i >= c_i) if fwd else (r_i <= c_i)).astype(BF16)
    lg = lg_ref[rows, :]
    lg1 = lg.astype(BF16)
    rem = lg - lg1.astype(F32)
    lg2 = rem.astype(BF16)
    lg3 = (rem - lg2.astype(F32)).astype(BF16)
    cum = _dot(tri, lg1) + (_dot(tri, lg2) + _dot(tri, lg3))
    yield
    tot = cum[C - 1:C, :] if fwd else cum[0:1, :]
    q = qk_ref[rows, :GLA_QK].astype(F32)
    k = qk_ref[rows, GLA_QK:].astype(F32)
    qt = (q * jnp.exp(cum)).astype(BF16)
    kt = (k * jnp.exp(-cum)).astype(BF16)
    kd = (k * jnp.exp(tot - cum)).astype(BF16)
    dec = jnp.exp(tot)
    mask = (r_i >= c_i) if fwd else (r_i < c_i)
    in_head = _head_lane_masks(C)
    zero = jnp.zeros((C, LANES), BF16)
    tiles = [slice(p * LANES, (p + 1) * LANES) for p in range(GLA_HEADS // GLA_HEADS_PER_TILE)]
    yield
    states, aos = [], []
    for p, tile in enumerate(tiles):
        s = s_ref[d, p]
        q_heads = jnp.concatenate([jnp.where(m, qt[:, tile], zero) for m in in_head], axis=0)
        aos.append(_dot_nt(q_heads, jnp.concatenate([kt[:, tile], s.astype(BF16)], axis=0)))
        states.append(s)
        yield
    for p, tile in enumerate(tiles):
        vt = [vt_ref[(p * GLA_HEADS_PER_TILE + hh) * GLA_DV:(p * GLA_HEADS_PER_TILE + hh + 1) * GLA_DV,
                     c * C:(c + 1) * C] for hh in range(GLA_HEADS_PER_TILE)]
        kd_heads = jnp.concatenate([jnp.where(m, kd[:, tile], zero) for m in in_head], axis=0)
        s_ref[d, p] = dec[:, tile] * states[p] + _dot(jnp.concatenate(vt, axis=1), kd_heads)
        yield
    for p in range(len(tiles)):
        for hh in range(GLA_HEADS_PER_TILE):
            h = p * GLA_HEADS_PER_TILE + hh
            vs = slice(h * GLA_DV, (h + 1) * GLA_DV)
            a = jnp.where(mask, aos[p][hh * C:(hh + 1) * C, :C], 0.0).astype(BF16)
            o_ref[rows, vs] = _dot(a, v_ref[rows, vs]) + aos[p][hh * C:(hh + 1) * C, C:]
            yield


GLA_STAGES_TO_STATE = 2 + 2 * (GLA_HEADS // GLA_HEADS_PER_TILE)


def _interleave(staggered):
    pending = sorted(staggered, key=lambda item: item[0])
    live, tick = [], 0
    while pending or live:
        while pending and pending[0][0] <= tick:
            live.append(pending.pop(0)[1])
        for g in list(live):
            try:
                next(g)
            except StopIteration:
                live.remove(g)
        tick += 1


def _gla_block_slow(qk_ref, v_ref, lg_ref, o_ref, s_ref, d, fwd, tb):
    G = 16

    def group(n, carry):
        rows = pl.ds(pl.multiple_of((n if fwd else tb // G - 1 - n) * G, G), G)
        gate = jnp.exp(lg_ref[rows, :])
        q = qk_ref[rows, :GLA_QK].astype(F32)
        k = qk_ref[rows, GLA_QK:]
        qs = (q if fwd else q * gate).astype(BF16)
        row = lax.broadcasted_iota(jnp.int32, (G, LANES), 0)
        in_head = _head_lane_masks(G)
        zero = jnp.zeros((G, LANES), BF16)
        for p in range(GLA_HEADS // GLA_HEADS_PER_TILE):
            tile = slice(p * LANES, (p + 1) * LANES)
            heads = [p * GLA_HEADS_PER_TILE + hh for hh in range(GLA_HEADS_PER_TILE)]
            vs = [slice(h * GLA_DV, (h + 1) * GLA_DV) for h in heads]
            v = [v_ref[rows, sl] for sl in vs]
            k_heads = [jnp.where(m, k[:, tile], zero) for m in in_head]
            s = s_ref[d, p]
            o = [jnp.zeros((G, GLA_DV), F32) for _ in heads]
            for r in (range(G) if fwd else reversed(range(G))):
                upd = None
                for hh in range(len(heads)):
                    term = _dot_tn(jnp.where(row == r, v[hh], zero), k_heads[hh])
                    upd = term if upd is None else upd + term
                s_new = gate[r:r + 1, tile] * s + upd
                s_read = (s_new if fwd else s).astype(BF16)
                for hh in range(len(heads)):
                    o[hh] = o[hh] + _dot_nt(jnp.where((row == r) & in_head[hh], qs[:, tile], zero), s_read)
                s = s_new
            for hh in range(len(heads)):
                o_ref[rows, vs[hh]] = o[hh]
            s_ref[d, p] = s
        return carry

    lax.fori_loop(0, tb // G, group, 0)


def _gla_kernel(qkf_ref, vf_ref, vtf_ref, lgf_ref, qkb_ref, vb_ref, vtb_ref, lgb_ref, of_ref, ob_ref, s_ref,
                *, groups, tb):
    i = pl.program_id(0)
    nblk = pl.num_programs(0)
    loc_f, _ = _seq_local(i, groups)
    loc_b, bps_b = _seq_local(nblk - 1 - i, groups)

    @pl.when(loc_f == 0)
    def _():
        s_ref[0] = jnp.zeros(s_ref.shape[1:], F32)

    @pl.when(loc_b == bps_b - 1)
    def _():
        s_ref[1] = jnp.zeros(s_ref.shape[1:], F32)

    nc = tb // GLA_CHUNK
    worst = jnp.float32(0.0)
    for ref in (lgf_ref, lgb_ref):
        for c in range(nc):
            tot = jnp.sum(ref[pl.ds(c * GLA_CHUNK, GLA_CHUNK), :], axis=0, keepdims=True)
            worst = jnp.minimum(worst, jnp.min(tot))
    safe = worst > -GLA_SAFE_LOG_DECAY

    @pl.when(safe)
    def _():
        work = []
        for c in range(nc):
            work.append((c * GLA_STAGES_TO_STATE,
                         _gla_chunk_fast(qkf_ref, vf_ref, vtf_ref, lgf_ref, of_ref, s_ref, 0, c, True)))
            work.append((c * GLA_STAGES_TO_STATE,
                         _gla_chunk_fast(qkb_ref, vb_ref, vtb_ref, lgb_ref, ob_ref, s_ref, 1, nc - 1 - c, False)))
        _interleave(work)

    @pl.when(jnp.logical_not(safe))
    def _():
        _gla_block_slow(qkf_ref, vf_ref, lgf_ref, of_ref, s_ref, 0, True, tb)
        _gla_block_slow(qkb_ref, vb_ref, lgb_ref, ob_ref, s_ref, 1, False, tb)


def _gla(qk, v, vt, lg, groups):
    n = qk.shape[0]
    tb = TOKEN_BLOCK
    nblk = n // tb
    fwd = lambda w, j: pl.BlockSpec((tb, w), lambda i: (i, j))
    bwd = lambda w, j: pl.BlockSpec((tb, w), lambda i: (nblk - 1 - i, j))
    return pl.pallas_call(
        functools.partial(_gla_kernel, groups=groups, tb=tb),
        grid=(nblk,),
        in_specs=[fwd(2 * GLA_QK, 0), fwd(GLA_V, 0), pl.BlockSpec((GLA_V, tb), lambda i: (0, i)), fwd(GLA_QK, 0),
                  bwd(2 * GLA_QK, 0), bwd(GLA_V, 0), pl.BlockSpec((GLA_V, tb), lambda i: (0, nblk - 1 - i)),
                  bwd(GLA_QK, 1)],
        out_specs=[fwd(GLA_V, 0), bwd(GLA_V, 0)],
        out_shape=[jax.ShapeDtypeStruct((n, GLA_V), F32)] * 2,
        scratch_shapes=[pltpu.VMEM((2, GLA_HEADS // GLA_HEADS_PER_TILE, GLA_DV, LANES), F32)],
        compiler_params=_cparams("arbitrary"),
        name="gla",
    )(qk, v, vt, lg, qk, v, vt, lg)


def _na_bias_table(rpb):
    w = GRID_W
    col = np.arange(w)
    start = np.clip(col - NA_KW // 2, 0, w - NA_KW)
    kc = np.arange(w)
    in_win = (kc[None, :] >= start[:, None]) & (kc[None, :] < start[:, None] + NA_KW)
    col_off = np.clip(kc[None, :] - col[:, None] + (NA_KW - 1), 0, 2 * NA_KW - 2)
    shift = np.arange(NA_KH)
    row_off = np.arange(NA_KH)[None, :] - shift[:, None] + (NA_KH - 1)
    t = rpb[:, row_off]
    t = t[:, :, :, col_off]
    t = jnp.where(jnp.asarray(in_win)[None, None, None], t, NEG_BIG)
    t = t.transpose(1, 0, 3, 2, 4)
    heads = LANES // NA_DH
    return t.reshape(NA_KH, NA_HEADS // heads, heads * w, NA_KH * w).astype(F32)


def _na_window_start(blk, tb, n_tokens):
    return jnp.clip((blk - 1) * tb, 0, n_tokens - 3 * tb)


def _na_kernel(q_ref, kwin_ref, vwin_ref, bias_ref, o_ref, *, groups, n_tokens):
    tb = TOKEN_BLOCK
    w = GRID_W
    rpb_rows = NA_ROWS_PER_BLOCK
    blk = pl.program_id(1)
    loc, bps = _seq_local(blk, groups)
    rows_in_seq = bps * rpb_rows
    seq_first = (blk - loc) * tb - _na_window_start(blk, tb, n_tokens)
    lane = lax.broadcasted_iota(jnp.int32, (w, LANES), 1)
    heads = LANES // NA_DH
    in_head = [(lane >= hh * NA_DH) & (lane < (hh + 1) * NA_DH) for hh in range(heads)]
    win = NA_KH * w

    def probs(j):
        r = loc * rpb_rows + j
        rs = jnp.clip(r - NA_KH // 2, 0, rows_in_seq - NA_KH)
        off = pl.multiple_of(seq_first + rs * w, w)
        q = q_ref[pl.ds(j * w, w), :]
        q_heads = jnp.concatenate([jnp.where(m, q, jnp.zeros_like(q)) for m in in_head], axis=0)
        s = _dot_nt(q_heads, kwin_ref[pl.ds(off, win), :]) + bias_ref[r - rs, 0]
        p = jnp.exp(s - jnp.max(s, axis=-1, keepdims=True))
        return j, off, p.astype(BF16), jnp.sum(p, axis=-1, keepdims=True)

    def output(j, off, p, l):
        o = _dot(p, vwin_ref[pl.ds(off, win), :]) * (1.0 / l)
        out = o[:w]
        for hh in range(1, heads):
            out = jnp.where(in_head[hh], o[hh * w:(hh + 1) * w], out)
        o_ref[pl.ds(j * w, w), :] = out.astype(o_ref.dtype)

    pending = []
    for j in range(rpb_rows):
        pending.append(probs(j))
        if len(pending) > 2:
            output(*pending.pop(0))
    for item in pending:
        output(*item)


def _na(q, k, v, bias, groups):
    n = q.shape[0]
    tb = TOKEN_BLOCK
    nblk = n // tb
    nhp = NA_W // LANES
    cur = pl.BlockSpec((tb, LANES), lambda hp, i: (i, hp))
    window = pl.BlockSpec((pl.Element(3 * tb), pl.Element(LANES)),
                          lambda hp, i: (pl.multiple_of(_na_window_start(i, tb, n), tb),
                                         pl.multiple_of(hp * LANES, LANES)))
    return pl.pallas_call(
        functools.partial(_na_kernel, groups=groups, n_tokens=n),
        grid=(nhp, nblk),
        in_specs=[cur, window, window,
                  pl.BlockSpec((NA_KH, 1) + bias.shape[2:], lambda hp, i: (0, hp, 0, 0))],
        out_specs=cur,
        out_shape=jax.ShapeDtypeStruct((n, NA_W), BF16),
        compiler_params=_cparams("parallel", "parallel"),
        name="natten",
    )(q, k, v, bias)


def _outproj_kernel(of_ref, ob_ref, gg_ref, ona_ref, x_ref, ng_ref, wog_ref, won_ref, g1_ref, b1_ref,
                    wrh_ref, wrl_ref, br_ref, x1_ref, e_ref, p_ref, pos_ref, cnt_ref, run_ref, ut_ref,
                    *, alpha):
    i = pl.program_id(0)
    tm = x_ref.shape[0]

    @pl.when(i == 0)
    def _():
        run_ref[...] = jnp.zeros(run_ref.shape, F32)
        r_i = lax.broadcasted_iota(jnp.int32, (tm, tm), 0)
        c_i = lax.broadcasted_iota(jnp.int32, (tm, tm), 1)
        ut_ref[...] = (r_i < c_i).astype(BF16)

    o = of_ref[...] + ob_ref[...]
    parts = []
    for h in range(GLA_HEADS):
        oh = o[:, h * GLA_DV:(h + 1) * GLA_DV]
        parts.append(oh * lax.rsqrt(jnp.mean(oh * oh, axis=-1, keepdims=True) + RMS_EPS))
    g = gg_ref[...]
    o = jnp.concatenate(parts, axis=-1) * ng_ref[...] * (g * jax.nn.sigmoid(g))
    mix = _dot(o.astype(BF16), wog_ref[...]) + _dot(ona_ref[...], won_ref[...])
    x1 = _layer_norm(alpha * x_ref[...] + mix, g1_ref[...], b1_ref[...])
    _slab_store(x1_ref, x1)

    x_hi = x1.astype(BF16)
    x_lo = (x1 - x_hi.astype(F32)).astype(BF16)
    logits = (_dot(x_hi, wrh_ref[...]) + (_dot(x_hi, wrl_ref[...]) + _dot(x_lo, wrh_ref[...])) + br_ref[...])
    lg = logits.T[:N_EXPERTS]
    eid = lax.broadcasted_iota(jnp.int32, (N_EXPERTS, tm), 0)
    out_row = lax.broadcasted_iota(jnp.int32, (8, tm), 0)
    chosen = jnp.zeros((N_EXPERTS, tm), F32)
    e_out = jnp.zeros((8, tm), jnp.int32)
    vals, sels = [], []
    for k in range(TOP_K):
        m = jnp.max(lg, axis=0, keepdims=True)
        idx = jnp.min(jnp.where(lg == m, eid, N_EXPERTS), axis=0, keepdims=True)
        sel = eid == idx
        sels.append(sel)
        vals.append(m)
        e_out = jnp.where(out_row == k, idx, e_out)
        chosen = jnp.where(sel, 1.0, chosen)
        lg = jnp.where(sel, -jnp.inf, lg)
    ex = [jnp.exp(v - vals[0]) for v in vals]
    inv = 1.0 / functools.reduce(lambda a, b: a + b, ex)
    p_out = jnp.zeros((8, tm), F32)
    for k in range(TOP_K):
        p_out = jnp.where(out_row == k, ex[k] * inv, p_out)
    p_ref[...] = p_out
    e_ref[...] = e_out
    run = run_ref[...]
    before = _dot(chosen.astype(BF16), ut_ref[...]) + run[:, :1]
    pos = jnp.zeros((8, tm), jnp.int32)
    for k in range(TOP_K):
        pk = jnp.sum(jnp.where(sels[k], before, 0.0), axis=0, keepdims=True)
        pos = jnp.where(out_row == k, pk.astype(jnp.int32), pos)
    pos_ref[...] = pos
    run = run + jnp.sum(chosen, axis=1, keepdims=True)
    run_ref[...] = run
    cnt_ref[...] = run.astype(jnp.int32)


def _outproj(o_f, o_b, gg, o_na, x, ng, wog, won, g1, b1, wrh, wrl, br, *, alpha):
    n, d = x.shape
    tm = TOKEN_BLOCK
    row = lambda w: pl.BlockSpec((tm, w), lambda i: (i, 0))
    col = pl.BlockSpec((8, tm), lambda i: (0, i))
    full = lambda a: pl.BlockSpec(a.shape, lambda i: (0,) * a.ndim)
    return pl.pallas_call(
        functools.partial(_outproj_kernel, alpha=alpha),
        grid=(n // tm,),
        in_specs=[row(GLA_V), row(GLA_V), row(GLA_V), row(NA_W), row(d), full(ng), full(wog), full(won),
                  full(g1), full(b1), full(wrh), full(wrl), full(br)],
        out_specs=[pl.BlockSpec((tm * d // LANES, LANES), lambda i: (i, 0)), col, col, col,
                   pl.BlockSpec((N_EXPERTS, LANES), lambda i: (0, 0))],
        out_shape=[jax.ShapeDtypeStruct((n * d // LANES, LANES), F32),
                   jax.ShapeDtypeStruct((8, n), jnp.int32), jax.ShapeDtypeStruct((8, n), F32),
                   jax.ShapeDtypeStruct((8, n), jnp.int32), jax.ShapeDtypeStruct((N_EXPERTS, LANES), jnp.int32)],
        scratch_shapes=[pltpu.VMEM((N_EXPERTS, LANES), F32), pltpu.VMEM((tm, tm), BF16)],
        compiler_params=_cparams("arbitrary"),
        name="outproj_router",
    )(o_f, o_b, gg, o_na, x, ng, wog, won, g1, b1, wrh, wrl, br)


def _slab_rows(ref, row, per):
    return ref.at[pl.ds(pl.multiple_of(row * per, per), per)]


DMA_ISSUE_UNROLL = 8


def _dispatch_kernel(zero_blk_ref, dest_ref, x_ref, xs_ref, zbuf, sem, *, per):
    tm = x_ref.shape[0] // per
    c = MOE_ROWS

    @pl.when(pl.program_id(0) == 0)
    def _():
        zbuf[...] = jnp.zeros(zbuf.shape, zbuf.dtype)

        def fill(j):
            return pltpu.make_async_copy(
                zbuf, xs_ref.at[pl.ds(pl.multiple_of(zero_blk_ref[j] * (c * per), c * per), c * per)], sem)

        def start(j, carry):
            @pl.when(zero_blk_ref[j] >= 0)
            def _():
                fill(j).start()
            return carry

        def wait(j, carry):
            @pl.when(zero_blk_ref[j] >= 0)
            def _():
                fill(j).wait()
            return carry

        lax.fori_loop(0, zero_blk_ref.shape[0], start, 0)
        lax.fori_loop(0, zero_blk_ref.shape[0], wait, 0)

    def group(g, carry):
        for u in range(DMA_ISSUE_UNROLL):
            t = g * DMA_ISSUE_UNROLL + u
            for k in range(TOP_K):
                pltpu.make_async_copy(_slab_rows(x_ref, t, per), _slab_rows(xs_ref, dest_ref[t * TOP_K + k], per),
                                      sem).start(priority=k % 2)
        return carry

    lax.fori_loop(0, tm // DMA_ISSUE_UNROLL, group, 0)
    for k in range(TOP_K):
        pltpu.make_async_copy(x_ref, xs_ref.at[pl.ds(0, tm * per)], sem).wait()


def _dispatch(zero_blk, dest_flat, x1_slab, n_slots, per):
    n = x1_slab.shape[0] // per
    tm = TOKEN_BLOCK
    grid_spec = pltpu.PrefetchScalarGridSpec(
        num_scalar_prefetch=1,
        grid=(n // tm,),
        in_specs=[pl.BlockSpec((tm * TOP_K,), lambda i, zb: (i,), memory_space=pltpu.SMEM),
                  pl.BlockSpec((tm * per, LANES), lambda i, zb: (i, 0))],
        out_specs=pl.BlockSpec(memory_space=pl.ANY),
        scratch_shapes=[pltpu.VMEM((MOE_ROWS * per, LANES), x1_slab.dtype), pltpu.SemaphoreType.DMA(())],
    )
    return pl.pallas_call(
        functools.partial(_dispatch_kernel, per=per),
        grid_spec=grid_spec,
        out_shape=jax.ShapeDtypeStruct((n_slots * per, LANES), x1_slab.dtype),
        compiler_params=_cparams("arbitrary"),
        name="moe_dispatch",
    )(zero_blk, dest_flat, x1_slab)


def _experts_kernel(be_ref, nact_ref, xs_ref, wgu_ref, bgu_ref, wd_ref, bd_ref, y_ref, wgu_bf, wd_bf):
    i = pl.program_id(0)
    d, de = wd_ref.shape[2], wd_ref.shape[1]
    c = MOE_ROWS

    @pl.when((i == 0) | (be_ref[i] != be_ref[jnp.maximum(i - 1, 0)]))
    def _():
        wgu_bf[...] = wgu_ref[0].astype(BF16)
        wd_bf[...] = wd_ref[0].astype(BF16)

    @pl.when(i < nact_ref[0])
    def _():
        x = _slab_load(xs_ref, c, d).astype(BF16)
        h = _dot(x, wgu_bf[...]) + bgu_ref[0]
        gate = jnp.minimum(h[:, :de], SWIGLU_LIMIT)
        up = jnp.clip(h[:, de:], -SWIGLU_LIMIT, SWIGLU_LIMIT)
        glu = gate * jax.nn.sigmoid(gate * SWIGLU_ALPHA)
        act = ((up + 1.0) * glu).astype(BF16)
        _slab_store(y_ref, _dot(act, wd_bf[...]) + bd_ref[0])

    @pl.when(i >= nact_ref[0])
    def _():
        y_ref[...] = jnp.zeros(y_ref.shape, y_ref.dtype)


def _experts(blk_e, nact, xs, layer, wgu, bgu, wd, bd):
    d, de = wd.shape[3], wd.shape[2]
    per = d // LANES
    c = MOE_ROWS
    slab = pl.BlockSpec((c * per, LANES), lambda i, be, na: (i, 0))
    grid_spec = pltpu.PrefetchScalarGridSpec(
        num_scalar_prefetch=2,
        grid=(xs.shape[0] // (c * per),),
        in_specs=[slab,
                  pl.BlockSpec((None, 1, d, 2 * de), lambda i, be, na: (layer, be[i], 0, 0)),
                  pl.BlockSpec((1, 1, 2 * de), lambda i, be, na: (be[i], 0, 0)),
                  pl.BlockSpec((None, 1, de, d), lambda i, be, na: (layer, be[i], 0, 0)),
                  pl.BlockSpec((1, 1, d), lambda i, be, na: (be[i], 0, 0))],
        out_specs=slab,
        scratch_shapes=[pltpu.VMEM((d, 2 * de), BF16), pltpu.VMEM((de, d), BF16)],
    )
    return pl.pallas_call(
        _experts_kernel,
        grid_spec=grid_spec,
        out_shape=jax.ShapeDtypeStruct(xs.shape, F32),
        compiler_params=_cparams("arbitrary"),
        name="moe_experts",
    )(blk_e, nact, xs, wgu, bgu, wd, bd)


def _final_kernel(*refs, ple_firsts, out_firsts, alpha):
    dest_ref, dest_next_ref, x1_ref, prob_ref = refs[:4]
    refs = refs[4:]
    ple_refs, refs = refs[:len(ple_firsts)], refs[len(ple_firsts):]
    y_ref, wpp_ref, wpg_ref, bpg_ref, g2_ref, b2_ref = refs[:6]
    refs = refs[6:]
    out_refs, scratch = refs[:len(out_firsts)], refs[len(out_firsts):]
    ybufs, sems = scratch[:TOP_K], scratch[TOP_K]
    tm, d = out_refs[0].shape
    per = d // LANES
    i = pl.program_id(0)
    slot = lax.rem(i, 2)

    def gather(idx_ref, into):
        def group(g, carry):
            for u in range(DMA_ISSUE_UNROLL):
                t = g * DMA_ISSUE_UNROLL + u
                for k in range(TOP_K):
                    pltpu.make_async_copy(_slab_rows(y_ref, idx_ref[t * TOP_K + k], per),
                                          _slab_rows(ybufs[k], into * tm + t, per),
                                          sems.at[into]).start(priority=k % 2)
            return carry
        lax.fori_loop(0, tm // DMA_ISSUE_UNROLL, group, 0)

    @pl.when(i == 0)
    def _():
        gather(dest_ref, slot)

    @pl.when(i + 1 < pl.num_programs(0))
    def _():
        gather(dest_next_ref, 1 - slot)

    half = pl.ds(pl.multiple_of(slot * (tm * per), tm * per), tm * per)
    for k in range(TOP_K):
        pltpu.make_async_copy(y_ref.at[pl.ds(0, tm * per)], ybufs[k].at[half], sems.at[slot]).wait()
    prob = prob_ref[...]
    moe = None
    for k in range(TOP_K):
        yk = _slab_load(ybufs[k].at[half], tm, d)
        moe = prob[:, k:k + 1] * yk if moe is None else moe + prob[:, k:k + 1] * yk
    r = alpha * _slab_load(x1_ref, tm, d) + moe
    gate = jax.nn.sigmoid(_dot(r.astype(BF16), wpg_ref[...]) + bpg_ref[...])
    ple = _select_part(i, ple_refs, ple_firsts)
    u = _dot(ple.astype(BF16), wpp_ref[...]) * gate
    res = _layer_norm(r + u, g2_ref[...], b2_ref[...])
    if len(out_refs) == 1:
        out_refs[0][...] = res
    else:
        bounds = list(out_firsts[1:]) + [pl.num_programs(0)]
        for ref, first, end in zip(out_refs, out_firsts, bounds):
            @pl.when((i >= first) & (i < end))
            def _(ref=ref):
                ref[...] = res


def _final(dest_flat, x1_slab, prob, ple_parts, layer, y_slab, wpp, wpg, bpg, g2, b2, out_rows, *, alpha):
    n = prob.shape[0]
    d = wpg.shape[0]
    per = d // LANES
    tm = FINAL_BLOCK
    nblk = n // tm
    row = lambda w: pl.BlockSpec((tm, w), lambda i: (i, 0))
    full = lambda a: pl.BlockSpec(a.shape, lambda i: (0,) * a.ndim)
    ple_specs, ple_firsts = _part_specs(ple_parts, tm, lead=layer)
    out_shapes = [jax.ShapeDtypeStruct((r, d), F32) for r in out_rows]
    out_specs, out_firsts = _part_specs(out_shapes, tm)
    return pl.pallas_call(
        functools.partial(_final_kernel, ple_firsts=ple_firsts, out_firsts=out_firsts, alpha=alpha),
        grid=(nblk,),
        in_specs=[pl.BlockSpec((tm * TOP_K,), lambda i: (i,), memory_space=pltpu.SMEM),
                  pl.BlockSpec((tm * TOP_K,), lambda i: (jnp.minimum(i + 1, nblk - 1),), memory_space=pltpu.SMEM),
                  pl.BlockSpec((tm * per, LANES), lambda i: (i, 0)), row(prob.shape[1])]
                 + ple_specs
                 + [pl.BlockSpec(memory_space=pl.ANY), full(wpp), full(wpg), full(bpg), full(g2), full(b2)],
        out_specs=out_specs,
        out_shape=out_shapes,
        scratch_shapes=[pltpu.VMEM((2 * tm * per, LANES), F32)] * TOP_K + [pltpu.SemaphoreType.DMA((2,))],
        compiler_params=_cparams("arbitrary"),
        name="moe_combine_final",
    )(dest_flat, dest_flat, x1_slab, prob, *ple_parts, y_slab, wpp, wpg, bpg, g2, b2)


def _routing_tables(top_e, pos, counts, n_blocks):
    c = MOE_ROWS
    e = jnp.arange(N_EXPERTS, dtype=jnp.int32)
    padded = (counts + c - 1) // c * c
    pad_end = jnp.sum(jnp.where(e[:, None] <= e[None, :], padded[:, None], 0), axis=0)
    pad_start = pad_end - padded
    start_of = jnp.sum(jnp.where(top_e[None] == e[:, None, None], pad_start[:, None, None], 0), axis=0)
    dest = (start_of + pos).T.reshape(-1)
    blk_first = jnp.arange(n_blocks, dtype=jnp.int32) * c
    blk_e = jnp.minimum(jnp.sum((pad_end[None, :] <= blk_first[:, None]).astype(jnp.int32), axis=1),
                        N_EXPERTS - 1)
    nact = pad_end[-1:] // c
    last_blk = jnp.where(padded > 0, pad_end // c - 1, -1)
    tail = nact + jnp.arange(n_blocks - top_e.size // c, dtype=jnp.int32)
    zero_blk = jnp.concatenate([last_blk, jnp.where(tail < n_blocks, tail, -1)])
    return dest.astype(jnp.int32), blk_e.astype(jnp.int32), nact.astype(jnp.int32), zero_blk.astype(jnp.int32)


def _layer(x_parts, ple_parts, lw, groups, out_rows, *, alpha, ln0):
    d = x_parts[0].shape[1]
    outs = _inproj(x_parts, ln0[0], ln0[1], lw['wg'], lw['wlr'], lw['wgk'], lw['bgk'], lw['wn'], apply_ln=ln0[2])
    if ln0[2]:
        x, outs = outs[0], outs[1:]
    else:
        (x,) = x_parts
    n = x.shape[0]
    qk, v, vt, gg, lg, qn, kn, vn = outs
    o_f, o_b = _gla(qk, v, vt, lg, groups)
    o_na = _na(qn, kn, vn, lw['na_bias'], groups)
    x1, top_e, prob, pos, cnt = _outproj(o_f, o_b, gg, o_na, x, lw['ng'], lw['wog'], lw['won'],
                                         lw['g1'], lw['b1'], lw['wrh'], lw['wrl'], lw['br'], alpha=alpha)
    n_blocks = (n * TOP_K + N_EXPERTS * (MOE_ROWS - 1) + MOE_ROWS - 1) // MOE_ROWS
    dest, blk_e, nact, zero_blk = _routing_tables(top_e[:TOP_K], pos[:TOP_K], cnt[:, 0], n_blocks)
    xs = _dispatch(zero_blk, dest, x1, n_blocks * MOE_ROWS, d // LANES)
    y = _experts(blk_e, nact, xs, lw['layer'], lw['wgu'], lw['bgu'], lw['wd'], lw['bd'])
    return _final(dest, x1, prob.T, ple_parts, lw['layer'], y, lw['wpp'], lw['wpg'], lw['bpg'], lw['g2'], lw['b2'],
                  out_rows, alpha=alpha)


def kernel(x_prompt, x_sample, p_prompt, p_sample, emb_ln_g, emb_ln_b, w_in, w_gk_f, b_gk_f, w_gk_b, b_gk_b, gla_norm_g, rpb, w_out, ln1_g, ln1_b, w_router, b_router, w_gu, b_gu, w_down, b_down, w_ple_proj, w_ple_gate, b_ple_gate, ln2_g, ln2_b):
    depth, d = w_in.shape[0], w_in.shape[1]
    alpha = float((2 * depth) ** 0.25)
    tb = TOKEN_BLOCK
    groups, first = [], 0
    for a in (x_prompt, x_sample):
        b, t, _ = a.shape
        assert t % tb == 0 and t % GRID_W == 0 and t // GRID_W >= NA_KH
        groups.append((first, t // tb))
        first += b * t // tb
    groups = tuple(groups)
    x_parts = [x_prompt.reshape(-1, d), x_sample.reshape(-1, d)]
    ple_parts = [p_prompt.reshape(depth, -1, p_prompt.shape[-1]), p_sample.reshape(depth, -1, p_sample.shape[-1])]
    n_all = sum(a.shape[0] for a in x_parts)
    row = lambda a: a.reshape(1, -1).astype(F32)
    c0, c1, c2 = 2 * GLA_QK + 2 * GLA_V, 2 * GLA_QK + 2 * GLA_V + 2 * GLA_RANK, w_in.shape[2]
    for i in range(depth):
        wgk = jnp.zeros((LANES, 2 * GLA_QK), F32)
        wgk = wgk.at[:GLA_RANK, :GLA_QK].set(w_gk_f[i]).at[GLA_RANK:2 * GLA_RANK, GLA_QK:].set(w_gk_b[i])
        wr = jnp.pad(w_router[i].astype(F32), ((0, 0), (0, LANES - N_EXPERTS)))
        lw = dict(
            wg=w_in[i, :, :c0].astype(BF16),
            wlr=jnp.pad(w_in[i, :, c0:c1], ((0, 0), (0, LANES - 2 * GLA_RANK))).astype(BF16),
            wgk=wgk.astype(BF16),
            bgk=jnp.concatenate([b_gk_f[i], b_gk_b[i]]).reshape(1, -1),
            wn=w_in[i, :, c1:c2].astype(BF16),
            na_bias=_na_bias_table(rpb[i]),
            ng=row(gla_norm_g[i]),
            wog=w_out[i, :GLA_V].astype(BF16), won=w_out[i, GLA_V:].astype(BF16),
            g1=row(ln1_g[i]), b1=row(ln1_b[i]),
            wrh=wr.astype(BF16), wrl=(wr - wr.astype(BF16).astype(F32)).astype(BF16),
            br=jnp.pad(b_router[i].astype(F32), (0, LANES - N_EXPERTS)).reshape(1, -1),
            layer=i, wgu=w_gu, bgu=b_gu[i][:, None, :], wd=w_down, bd=b_down[i][:, None, :],
            wpp=w_ple_proj[i].astype(BF16), wpg=w_ple_gate[i].astype(BF16), bpg=row(b_ple_gate[i]),
            g2=row(ln2_g[i]), b2=row(ln2_b[i]),
        )
        last = i == depth - 1
        out_rows = [a.shape[0] for a in x_parts] if last else [n_all]
        outs = _layer(x_parts if i == 0 else [x], ple_parts, lw, groups, out_rows,
                      alpha=alpha, ln0=(row(emb_ln_g), row(emb_ln_b), i == 0))
        if not last:
            (x,) = outs
    return (outs[0].reshape(x_prompt.shape), outs[1].reshape(x_sample.shape))
```

```python
import functools

import numpy as np
import jax
import jax.numpy as jnp
from jax import lax
from jax.experimental import pallas as pl
from jax.experimental.pallas import tpu as pltpu

GRID_W = 64
GLA_HEADS, GLA_DK, GLA_DV, GLA_RANK = 4, 64, 128, 16
GLA_NORMALIZER = 16.0
NA_HEADS, NA_DH, NA_KH, NA_KW = 8, 64, 8, 16
N_EXPERTS, TOP_K = 32, 4
SWIGLU_ALPHA, SWIGLU_LIMIT = 1.702, 7.0
LN_EPS, RMS_EPS = 1e-5, 1e-6

GLA_QK = GLA_HEADS * GLA_DK
GLA_V = GLA_HEADS * GLA_DV
NA_W = NA_HEADS * NA_DH

LANES = 128
V7X_VMEM_LIMIT_BYTES = 56 * 1024 * 1024

TOKEN_BLOCK = 512
GLA_CHUNK = 128
GLA_SAFE_LOG_DECAY = 40.0
NA_ROWS_PER_BLOCK = TOKEN_BLOCK // GRID_W
MOE_ROWS = 1024
FINAL_BLOCK = 256
NEG_BIG = -1e30

F32 = jnp.float32
BF16 = jnp.bfloat16


def _cparams(*sem):
    return pltpu.CompilerParams(dimension_semantics=sem, vmem_limit_bytes=V7X_VMEM_LIMIT_BYTES)


def _dot(a, b, precision=None):
    return jnp.dot(a, b, preferred_element_type=F32, precision=precision)


def _dot_nt(a, b, precision=None):
    return lax.dot_general(a, b, (((1,), (1,)), ((), ())), preferred_element_type=F32, precision=precision)


def _dot_tn(a, b):
    return lax.dot_general(a, b, (((0,), (0,)), ((), ())), preferred_element_type=F32)


def _layer_norm(x, g, b):
    mu = jnp.mean(x, axis=-1, keepdims=True)
    xc = x - mu
    var = jnp.mean(xc * xc, axis=-1, keepdims=True)
    return xc * lax.rsqrt(var + LN_EPS) * g + b


def _slab_load(ref, n, width):
    per = width // LANES
    return jnp.concatenate([ref[pl.ds(s, n, stride=per), :] for s in range(per)], axis=-1)


def _slab_store(ref, x):
    n, width = x.shape
    per = width // LANES
    for s in range(per):
        ref[pl.ds(s, n, stride=per), :] = x[:, s * LANES:(s + 1) * LANES]


def _seq_local(blk, groups):
    loc = lax.rem(blk - groups[-1][0], groups[-1][1])
    bps = jnp.int32(groups[-1][1])
    for (first, per), nxt in zip(reversed(groups[:-1]), reversed(groups[1:])):
        inside = blk < nxt[0]
        loc = jnp.where(inside, lax.rem(blk - first, per), loc)
        bps = jnp.where(inside, per, bps)
    return loc, bps


def _part_specs(parts, block_rows, lead=None):
    specs, firsts, first = [], [], 0
    for a in parts:
        rows, width = a.shape[-2], a.shape[-1]
        nb = rows // block_rows
        if lead is None:
            spec = pl.BlockSpec((block_rows, width), lambda i, first=first, nb=nb: (jnp.clip(i - first, 0, nb - 1), 0))
        else:
            spec = pl.BlockSpec((None, block_rows, width),
                                lambda i, first=first, nb=nb: (lead, jnp.clip(i - first, 0, nb - 1), 0))
        specs.append(spec)
        firsts.append(first)
        first += nb
    return specs, tuple(firsts)


def _select_part(i, refs, firsts):
    x = refs[0][...]
    for ref, first in zip(refs[1:], firsts[1:]):
        x = jnp.where(i >= first, ref[...], x)
    return x


def _inproj_kernel(*refs, firsts, apply_ln):
    x_refs, refs = refs[:len(firsts)], refs[len(firsts):]
    g_ref, b_ref, wg_ref, wlr_ref, wgk_ref, bgk_ref, wn_ref = refs[:7]
    out_refs = refs[7:]
    if apply_ln:
        x0_ref, qk_ref, v_ref, vt_ref, gg_ref, lg_ref, qn_ref, kn_ref, vn_ref = out_refs
    else:
        qk_ref, v_ref, vt_ref, gg_ref, lg_ref, qn_ref, kn_ref, vn_ref = out_refs
    x = _select_part(pl.program_id(0), x_refs, firsts)
    if apply_ln:
        x = _layer_norm(x, g_ref[...], b_ref[...])
        x0_ref[...] = x
    xb = x.astype(BF16)
    zg = _dot(xb, wg_ref[...])
    qk_ref[:, :GLA_QK] = (zg[:, :GLA_QK] * GLA_DK ** -0.5).astype(BF16)
    qk_ref[:, GLA_QK:] = zg[:, GLA_QK:2 * GLA_QK].astype(BF16)
    v = zg[:, 2 * GLA_QK:2 * GLA_QK + GLA_V]
    v_ref[...] = v.astype(BF16)
    vt_ref[...] = v.T.astype(BF16)
    gg_ref[...] = zg[:, 2 * GLA_QK + GLA_V:]
    lr = _dot(xb, wlr_ref[...])
    pre = _dot(lr.astype(BF16), wgk_ref[...]) + bgk_ref[...]
    log_sig = jnp.minimum(pre, 0.0) - jnp.log1p(jnp.exp(-jnp.abs(pre)))
    lg_ref[...] = log_sig * (1.0 / GLA_NORMALIZER)
    zn = _dot(xb, wn_ref[...])
    qn_ref[...] = (zn[:, :NA_W] * NA_DH ** -0.5).astype(BF16)
    kn_ref[...] = zn[:, NA_W:2 * NA_W].astype(BF16)
    vn_ref[...] = zn[:, 2 * NA_W:].astype(BF16)


def _inproj(x_parts, ln_g, ln_b, wg, wlr, wgk, bgk, wn, *, apply_ln):
    n = sum(a.shape[0] for a in x_parts)
    d = x_parts[0].shape[1]
    tm = TOKEN_BLOCK
    row = lambda w: pl.BlockSpec((tm, w), lambda i: (i, 0))
    full = lambda a: pl.BlockSpec(a.shape, lambda i: (0,) * a.ndim)
    x_specs, firsts = _part_specs(x_parts, tm)
    outs = [(GLA_QK * 2, BF16, False), (GLA_V, BF16, False), (GLA_V, BF16, True), (GLA_V, F32, False),
            (2 * GLA_QK, F32, False), (NA_W, BF16, False), (NA_W, BF16, False), (NA_W, BF16, False)]
    if apply_ln:
        outs = [(d, F32, False)] + outs
    return pl.pallas_call(
        functools.partial(_inproj_kernel, firsts=firsts, apply_ln=apply_ln),
        grid=(n // tm,),
        in_specs=x_specs + [full(ln_g), full(ln_b), full(wg), full(wlr), full(wgk), full(bgk), full(wn)],
        out_specs=[pl.BlockSpec((w, tm), lambda i: (0, i)) if tr else row(w) for w, _, tr in outs],
        out_shape=[jax.ShapeDtypeStruct((w, n) if tr else (n, w), t) for w, t, tr in outs],
        compiler_params=_cparams("arbitrary"),
        name="inproj_ln" if apply_ln else "inproj",
    )(*x_parts, ln_g, ln_b, wg, wlr, wgk, bgk, wn)


GLA_HEADS_PER_TILE = LANES // GLA_DK
assert GLA_DV == LANES and GLA_HEADS % GLA_HEADS_PER_TILE == 0


def _head_lane_masks(rows):
    lane = lax.broadcasted_iota(jnp.int32, (rows, LANES), 1)
    return [(lane >= hh * GLA_DK) & (lane < (hh + 1) * GLA_DK) for hh in range(GLA_HEADS_PER_TILE)]


def _gla_chunk_fast(qk_ref, v_ref, vt_ref, lg_ref, o_ref, s_ref, d, c, fwd):
    C = GLA_CHUNK
    rows = pl.ds(c * C, C)
    r_i = lax.broadcasted_iota(jnp.int32, (C, C), 0)
    c_i = lax.broadcasted_iota(jnp.int32, (C, C), 1)
    tri = ((r_i >= c_i) if fwd else (r_i <= c_i)).astype(BF16)
    lg = lg_ref[rows, :]
    lg1 = lg.astype(BF16)
    rem = lg - lg1.astype(F32)
    lg2 = rem.astype(BF16)
    lg3 = (rem - lg2.astype(F32)).astype(BF16)
    cum = _dot(tri, lg1) + (_dot(tri, lg2) + _dot(tri, lg3))
    yield
    tot = cum[C - 1:C, :] if fwd else cum[0:1, :]
    q = qk_ref[rows, :GLA_QK].astype(F32)
    k = qk_ref[rows, GLA_QK:].astype(F32)
    qt = (q * jnp.exp(cum)).astype(BF16)
    kt = (k * jnp.exp(-cum)).astype(BF16)
    kd = (k * jnp.exp(tot - cum)).astype(BF16)
    dec = jnp.exp(tot)
    mask = (r_i >= c_i) if fwd else (r_i < c_i)
    in_head = _head_lane_masks(C)
    zero = jnp.zeros((C, LANES), BF16)
    tiles = [slice(p * LANES, (p + 1) * LANES) for p in range(GLA_HEADS // GLA_HEADS_PER_TILE)]
    yield
    states, aos = [], []
    for p, tile in enumerate(tiles):
        s = s_ref[d, p]
        q_heads = jnp.concatenate([jnp.where(m, qt[:, tile], zero) for m in in_head], axis=0)
        aos.append(_dot_nt(q_heads, jnp.concatenate([kt[:, tile], s.astype(BF16)], axis=0)))
        states.append(s)
        yield
    for p, tile in enumerate(tiles):
        vt = [vt_ref[(p * GLA_HEADS_PER_TILE + hh) * GLA_DV:(p * GLA_HEADS_PER_TILE + hh + 1) * GLA_DV,
                     c * C:(c + 1) * C] for hh in range(GLA_HEADS_PER_TILE)]
        kd_heads = jnp.concatenate([jnp.where(m, kd[:, tile], zero) for m in in_head], axis=0)
        s_ref[d, p] = dec[:, tile] * states[p] + _dot(jnp.concatenate(vt, axis=1), kd_heads)
        yield
    for p in range(len(tiles)):
        for hh in range(GLA_HEADS_PER_TILE):
            h = p * GLA_HEADS_PER_TILE + hh
            vs = slice(h * GLA_DV, (h + 1) * GLA_DV)
            a = jnp.where(mask, aos[p][hh * C:(hh + 1) * C, :C], 0.0).astype(BF16)
            o_ref[rows, vs] = _dot(a, v_ref[rows, vs]) + aos[p][hh * C:(hh + 1) * C, C:]
            yield


GLA_STAGES_TO_STATE = 2 + 2 * (GLA_HEADS // GLA_HEADS_PER_TILE)


def _interleave(staggered):
    pending = sorted(staggered, key=lambda item: item[0])
    live, tick = [], 0
    while pending or live:
        while pending and pending[0][0] <= tick:
            live.append(pending.pop(0)[1])
        for g in list(live):
            try:
                next(g)
            except StopIteration:
                live.remove(g)
        tick += 1


def _gla_block_slow(qk_ref, v_ref, lg_ref, o_ref, s_ref, d, fwd, tb):
    G = 16

    def group(n, carry):
        rows = pl.ds(pl.multiple_of((n if fwd else tb // G - 1 - n) * G, G), G)
        gate = jnp.exp(lg_ref[rows, :])
        q = qk_ref[rows, :GLA_QK].astype(F32)
        k = qk_ref[rows, GLA_QK:]
        qs = (q if fwd else q * gate).astype(BF16)
        row = lax.broadcasted_iota(jnp.int32, (G, LANES), 0)
        in_head = _head_lane_masks(G)
        zero = jnp.zeros((G, LANES), BF16)
        for p in range(GLA_HEADS // GLA_HEADS_PER_TILE):
            tile = slice(p * LANES, (p + 1) * LANES)
            heads = [p * GLA_HEADS_PER_TILE + hh for hh in range(GLA_HEADS_PER_TILE)]
            vs = [slice(h * GLA_DV, (h + 1) * GLA_DV) for h in heads]
            v = [v_ref[rows, sl] for sl in vs]
            k_heads = [jnp.where(m, k[:, tile], zero) for m in in_head]
            s = s_ref[d, p]
            o = [jnp.zeros((G, GLA_DV), F32) for _ in heads]
            for r in (range(G) if fwd else reversed(range(G))):
                upd = None
                for hh in range(len(heads)):
                    term = _dot_tn(jnp.where(row == r, v[hh], zero), k_heads[hh])
                    upd = term if upd is None else upd + term
                s_new = gate[r:r + 1, tile] * s + upd
                s_read = (s_new if fwd else s).astype(BF16)
                for hh in range(len(heads)):
                    o[hh] = o[hh] + _dot_nt(jnp.where((row == r) & in_head[hh], qs[:, tile], zero), s_read)
                s = s_new
            for hh in range(len(heads)):
                o_ref[rows, vs[hh]] = o[hh]
            s_ref[d, p] = s
        return carry

    lax.fori_loop(0, tb // G, group, 0)


def _gla_kernel(qkf_ref, vf_ref, vtf_ref, lgf_ref, qkb_ref, vb_ref, vtb_ref, lgb_ref, of_ref, ob_ref, s_ref,
                *, groups, tb):
    i = pl.program_id(0)
    nblk = pl.num_programs(0)
    loc_f, _ = _seq_local(i, groups)
    loc_b, bps_b = _seq_local(nblk - 1 - i, groups)

    @pl.when(loc_f == 0)
    def _():
        s_ref[0] = jnp.zeros(s_ref.shape[1:], F32)

    @pl.when(loc_b == bps_b - 1)
    def _():
        s_ref[1] = jnp.zeros(s_ref.shape[1:], F32)

    nc = tb // GLA_CHUNK
    worst = jnp.float32(0.0)
    for ref in (lgf_ref, lgb_ref):
        for c in range(nc):
            tot = jnp.sum(ref[pl.ds(c * GLA_CHUNK, GLA_CHUNK), :], axis=0, keepdims=True)
            worst = jnp.minimum(worst, jnp.min(tot))
    safe = worst > -GLA_SAFE_LOG_DECAY

    @pl.when(safe)
    def _():
        work = []
        for c in range(nc):
            work.append((c * GLA_STAGES_TO_STATE,
                         _gla_chunk_fast(qkf_ref, vf_ref, vtf_ref, lgf_ref, of_ref, s_ref, 0, c, True)))
            work.append((c * GLA_STAGES_TO_STATE,
                         _gla_chunk_fast(qkb_ref, vb_ref, vtb_ref, lgb_ref, ob_ref, s_ref, 1, nc - 1 - c, False)))
        _interleave(work)

    @pl.when(jnp.logical_not(safe))
    def _():
        _gla_block_slow(qkf_ref, vf_ref, lgf_ref, of_ref, s_ref, 0, True, tb)
        _gla_block_slow(qkb_ref, vb_ref, lgb_ref, ob_ref, s_ref, 1, False, tb)


def _gla(qk, v, vt, lg, groups):
    n = qk.shape[0]
    tb = TOKEN_BLOCK
    nblk = n // tb
    fwd = lambda w, j: pl.BlockSpec((tb, w), lambda i: (i, j))
    bwd = lambda w, j: pl.BlockSpec((tb, w), lambda i: (nblk - 1 - i, j))
    return pl.pallas_call(
        functools.partial(_gla_kernel, groups=groups, tb=tb),
        grid=(nblk,),
        in_specs=[fwd(2 * GLA_QK, 0), fwd(GLA_V, 0), pl.BlockSpec((GLA_V, tb), lambda i: (0, i)), fwd(GLA_QK, 0),
                  bwd(2 * GLA_QK, 0), bwd(GLA_V, 0), pl.BlockSpec((GLA_V, tb), lambda i: (0, nblk - 1 - i)),
                  bwd(GLA_QK, 1)],
        out_specs=[fwd(GLA_V, 0), bwd(GLA_V, 0)],
        out_shape=[jax.ShapeDtypeStruct((n, GLA_V), F32)] * 2,
        scratch_shapes=[pltpu.VMEM((2, GLA_HEADS // GLA_HEADS_PER_TILE, GLA_DV, LANES), F32)],
        compiler_params=_cparams("arbitrary"),
        name="gla",
    )(qk, v, vt, lg, qk, v, vt, lg)


def _na_bias_table(rpb):
    w = GRID_W
    col = np.arange(w)
    start = np.clip(col - NA_KW // 2, 0, w - NA_KW)
    kc = np.arange(w)
    in_win = (kc[None, :] >= start[:, None]) & (kc[None, :] < start[:, None] + NA_KW)
    t = jnp.stack([rpb[:, NA_KH - 1 - s:2 * NA_KH - 1 - s] for s in range(NA_KH)], axis=1)
    t = jnp.pad(t, ((0, 0), (0, 0), (0, 0), (w, w)))
    t = jnp.stack([t[..., w + NA_KW - 1 - c:2 * w + NA_KW - 1 - c] for c in range(w)], axis=3)
    t = jnp.where(jnp.asarray(in_win)[None, None, None], t, NEG_BIG)
    t = t.transpose(1, 0, 3, 2, 4)
    heads = LANES // NA_DH
    return t.reshape(NA_KH, NA_HEADS // heads, heads * w, NA_KH * w).astype(F32)


def _na_window_start(blk, tb, n_tokens):
    return jnp.clip((blk - 1) * tb, 0, n_tokens - 3 * tb)


def _na_kernel(q_ref, kwin_ref, vwin_ref, bias_ref, o_ref, *, groups, n_tokens):
    tb = TOKEN_BLOCK
    w = GRID_W
    rpb_rows = NA_ROWS_PER_BLOCK
    blk = pl.program_id(1)
    loc, bps = _seq_local(blk, groups)
    rows_in_seq = bps * rpb_rows
    seq_first = (blk - loc) * tb - _na_window_start(blk, tb, n_tokens)
    lane = lax.broadcasted_iota(jnp.int32, (w, LANES), 1)
    heads = LANES // NA_DH
    in_head = [(lane >= hh * NA_DH) & (lane < (hh + 1) * NA_DH) for hh in range(heads)]
    win = NA_KH * w

    def probs(j):
        r = loc * rpb_rows + j
        rs = jnp.clip(r - NA_KH // 2, 0, rows_in_seq - NA_KH)
        off = pl.multiple_of(seq_first + rs * w, w)
        q = q_ref[pl.ds(j * w, w), :]
        q_heads = jnp.concatenate([jnp.where(m, q, jnp.zeros_like(q)) for m in in_head], axis=0)
        s = _dot_nt(q_heads, kwin_ref[pl.ds(off, win), :]) + bias_ref[r - rs, 0]
        p = jnp.exp(s - jnp.max(s, axis=-1, keepdims=True))
        return j, off, p.astype(BF16), jnp.sum(p, axis=-1, keepdims=True)

    def output(j, off, p, l):
        o = _dot(p, vwin_ref[pl.ds(off, win), :]) * (1.0 / l)
        out = o[:w]
        for hh in range(1, heads):
            out = jnp.where(in_head[hh], o[hh * w:(hh + 1) * w], out)
        o_ref[pl.ds(j * w, w), :] = out.astype(o_ref.dtype)

    pending = []
    for j in range(rpb_rows):
        pending.append(probs(j))
        if len(pending) > 2:
            output(*pending.pop(0))
    for item in pending:
        output(*item)


def _na(q, k, v, bias, groups):
    n = q.shape[0]
    tb = TOKEN_BLOCK
    nblk = n // tb
    nhp = NA_W // LANES
    cur = pl.BlockSpec((tb, LANES), lambda hp, i: (i, hp))
    window = pl.BlockSpec((pl.Element(3 * tb), pl.Element(LANES)),
                          lambda hp, i: (pl.multiple_of(_na_window_start(i, tb, n), tb),
                                         pl.multiple_of(hp * LANES, LANES)))
    return pl.pallas_call(
        functools.partial(_na_kernel, groups=groups, n_tokens=n),
        grid=(nhp, nblk),
        in_specs=[cur, window, window,
                  pl.BlockSpec((NA_KH, 1) + bias.shape[2:], lambda hp, i: (0, hp, 0, 0))],
        out_specs=cur,
        out_shape=jax.ShapeDtypeStruct((n, NA_W), BF16),
        compiler_params=_cparams("parallel", "parallel"),
        name="natten",
    )(q, k, v, bias)


def _outproj_kernel(of_ref, ob_ref, gg_ref, ona_ref, x_ref, ng_ref, wog_ref, won_ref, g1_ref, b1_ref,
                    wrh_ref, wrl_ref, br_ref, x1_ref, e_ref, p_ref, pos_ref, cnt_ref, run_ref, ut_ref,
                    *, alpha):
    i = pl.program_id(0)
    tm = x_ref.shape[0]

    @pl.when(i == 0)
    def _():
        run_ref[...] = jnp.zeros(run_ref.shape, F32)
        r_i = lax.broadcasted_iota(jnp.int32, (tm, tm), 0)
        c_i = lax.broadcasted_iota(jnp.int32, (tm, tm), 1)
        ut_ref[...] = (r_i < c_i).astype(BF16)

    o = of_ref[...] + ob_ref[...]
    parts = []
    for h in range(GLA_HEADS):
        oh = o[:, h * GLA_DV:(h + 1) * GLA_DV]
        parts.append(oh * lax.rsqrt(jnp.mean(oh * oh, axis=-1, keepdims=True) + RMS_EPS))
    g = gg_ref[...]
    o = jnp.concatenate(parts, axis=-1) * ng_ref[...] * (g * jax.nn.sigmoid(g))
    mix = _dot(o.astype(BF16), wog_ref[...]) + _dot(ona_ref[...], won_ref[...])
    x1 = _layer_norm(alpha * x_ref[...] + mix, g1_ref[...], b1_ref[...])
    _slab_store(x1_ref, x1)

    x_hi = x1.astype(BF16)
    x_lo = (x1 - x_hi.astype(F32)).astype(BF16)
    logits = (_dot(x_hi, wrh_ref[...]) + (_dot(x_hi, wrl_ref[...]) + _dot(x_lo, wrh_ref[...])) + br_ref[...])
    lg = logits.T[:N_EXPERTS]
    eid = lax.broadcasted_iota(jnp.int32, (N_EXPERTS, tm), 0)
    out_row = lax.broadcasted_iota(jnp.int32, (8, tm), 0)
    chosen = jnp.zeros((N_EXPERTS, tm), F32)
    e_out = jnp.zeros((8, tm), jnp.int32)
    vals, sels = [], []
    for k in range(TOP_K):
        m = jnp.max(lg, axis=0, keepdims=True)
        idx = jnp.min(jnp.where(lg == m, eid, N_EXPERTS), axis=0, keepdims=True)
        sel = eid == idx
        sels.append(sel)
        vals.append(m)
        e_out = jnp.where(out_row == k, idx, e_out)
        chosen = jnp.where(sel, 1.0, chosen)
        lg = jnp.where(sel, -jnp.inf, lg)
    ex = [jnp.exp(v - vals[0]) for v in vals]
    inv = 1.0 / functools.reduce(lambda a, b: a + b, ex)
    p_out = jnp.zeros((8, tm), F32)
    for k in range(TOP_K):
        p_out = jnp.where(out_row == k, ex[k] * inv, p_out)
    p_ref[...] = p_out
    e_ref[...] = e_out
    run = run_ref[...]
    before = _dot(chosen.astype(BF16), ut_ref[...]) + run[:, :1]
    pos = jnp.zeros((8, tm), jnp.int32)
    for k in range(TOP_K):
        pk = jnp.sum(jnp.where(sels[k], before, 0.0), axis=0, keepdims=True)
        pos = jnp.where(out_row == k, pk.astype(jnp.int32), pos)
    pos_ref[...] = pos
    run = run + jnp.sum(chosen, axis=1, keepdims=True)
    run_ref[...] = run
    cnt_ref[...] = run.astype(jnp.int32)


def _outproj(o_f, o_b, gg, o_na, x, ng, wog, won, g1, b1, wrh, wrl, br, *, alpha):
    n, d = x.shape
    tm = TOKEN_BLOCK
    row = lambda w: pl.BlockSpec((tm, w), lambda i: (i, 0))
    col = pl.BlockSpec((8, tm), lambda i: (0, i))
    full = lambda a: pl.BlockSpec(a.shape, lambda i: (0,) * a.ndim)
    return pl.pallas_call(
        functools.partial(_outproj_kernel, alpha=alpha),
        grid=(n // tm,),
        in_specs=[row(GLA_V), row(GLA_V), row(GLA_V), row(NA_W), row(d), full(ng), full(wog), full(won),
                  full(g1), full(b1), full(wrh), full(wrl), full(br)],
        out_specs=[pl.BlockSpec((tm * d // LANES, LANES), lambda i: (i, 0)), col, col, col,
                   pl.BlockSpec((N_EXPERTS, LANES), lambda i: (0, 0))],
        out_shape=[jax.ShapeDtypeStruct((n * d // LANES, LANES), F32),
                   jax.ShapeDtypeStruct((8, n), jnp.int32), jax.ShapeDtypeStruct((8, n), F32),
                   jax.ShapeDtypeStruct((8, n), jnp.int32), jax.ShapeDtypeStruct((N_EXPERTS, LANES), jnp.int32)],
        scratch_shapes=[pltpu.VMEM((N_EXPERTS, LANES), F32), pltpu.VMEM((tm, tm), BF16)],
        compiler_params=_cparams("arbitrary"),
        name="outproj_router",
    )(o_f, o_b, gg, o_na, x, ng, wog, won, g1, b1, wrh, wrl, br)


def _slab_rows(ref, row, per):
    return ref.at[pl.ds(pl.multiple_of(row * per, per), per)]


DMA_ISSUE_UNROLL = 8


def _dispatch_kernel(zero_blk_ref, dest_ref, x_ref, xs_ref, zbuf, sem, *, per):
    tm = x_ref.shape[0] // per
    c = MOE_ROWS

    @pl.when(pl.program_id(0) == 0)
    def _():
        zbuf[...] = jnp.zeros(zbuf.shape, zbuf.dtype)

        def fill(j):
            return pltpu.make_async_copy(
                zbuf, xs_ref.at[pl.ds(pl.multiple_of(zero_blk_ref[j] * (c * per), c * per), c * per)], sem)

        def start(j, carry):
            @pl.when(zero_blk_ref[j] >= 0)
            def _():
                fill(j).start()
            return carry

        def wait(j, carry):
            @pl.when(zero_blk_ref[j] >= 0)
            def _():
                fill(j).wait()
            return carry

        lax.fori_loop(0, zero_blk_ref.shape[0], start, 0)
        lax.fori_loop(0, zero_blk_ref.shape[0], wait, 0)

    def group(g, carry):
        for u in range(DMA_ISSUE_UNROLL):
            t = g * DMA_ISSUE_UNROLL + u
            for k in range(TOP_K):
                pltpu.make_async_copy(_slab_rows(x_ref, t, per), _slab_rows(xs_ref, dest_ref[t * TOP_K + k], per),
                                      sem).start(priority=k % 2)
        return carry

    lax.fori_loop(0, tm // DMA_ISSUE_UNROLL, group, 0)
    for k in range(TOP_K):
        pltpu.make_async_copy(x_ref, xs_ref.at[pl.ds(0, tm * per)], sem).wait()


def _dispatch(zero_blk, dest_flat, x1_slab, n_slots, per):
    n = x1_slab.shape[0] // per
    tm = TOKEN_BLOCK
    grid_spec = pltpu.PrefetchScalarGridSpec(
        num_scalar_prefetch=1,
        grid=(n // tm,),
        in_specs=[pl.BlockSpec((tm * TOP_K,), lambda i, zb: (i,), memory_space=pltpu.SMEM),
                  pl.BlockSpec((tm * per, LANES), lambda i, zb: (i, 0))],
        out_specs=pl.BlockSpec(memory_space=pl.ANY),
        scratch_shapes=[pltpu.VMEM((MOE_ROWS * per, LANES), x1_slab.dtype), pltpu.SemaphoreType.DMA(())],
    )
    return pl.pallas_call(
        functools.partial(_dispatch_kernel, per=per),
        grid_spec=grid_spec,
        out_shape=jax.ShapeDtypeStruct((n_slots * per, LANES), x1_slab.dtype),
        compiler_params=_cparams("arbitrary"),
        name="moe_dispatch",
    )(zero_blk, dest_flat, x1_slab)


def _experts_kernel(be_ref, nact_ref, xs_ref, wgu_ref, bgu_ref, wd_ref, bd_ref, y_ref, wgu_bf, wd_bf):
    i = pl.program_id(0)
    d, de = wd_ref.shape[2], wd_ref.shape[1]
    c = MOE_ROWS

    @pl.when((i == 0) | (be_ref[i] != be_ref[jnp.maximum(i - 1, 0)]))
    def _():
        wgu_bf[...] = wgu_ref[0].astype(BF16)
        wd_bf[...] = wd_ref[0].astype(BF16)

    @pl.when(i < nact_ref[0])
    def _():
        x = _slab_load(xs_ref, c, d).astype(BF16)
        h = _dot(x, wgu_bf[...]) + bgu_ref[0]
        gate = jnp.minimum(h[:, :de], SWIGLU_LIMIT)
        up = jnp.clip(h[:, de:], -SWIGLU_LIMIT, SWIGLU_LIMIT)
        glu = gate * jax.nn.sigmoid(gate * SWIGLU_ALPHA)
        act = ((up + 1.0) * glu).astype(BF16)
        _slab_store(y_ref, _dot(act, wd_bf[...]) + bd_ref[0])

    @pl.when(i >= nact_ref[0])
    def _():
        y_ref[...] = jnp.zeros(y_ref.shape, y_ref.dtype)


def _experts(blk_e, nact, xs, layer, wgu, bgu, wd, bd):
    d, de = wd.shape[3], wd.shape[2]
    per = d // LANES
    c = MOE_ROWS
    slab = pl.BlockSpec((c * per, LANES), lambda i, be, na: (i, 0))
    grid_spec = pltpu.PrefetchScalarGridSpec(
        num_scalar_prefetch=2,
        grid=(xs.shape[0] // (c * per),),
        in_specs=[slab,
                  pl.BlockSpec((None, 1, d, 2 * de), lambda i, be, na: (layer, be[i], 0, 0)),
                  pl.BlockSpec((1, 1, 2 * de), lambda i, be, na: (be[i], 0, 0)),
                  pl.BlockSpec((None, 1, de, d), lambda i, be, na: (layer, be[i], 0, 0)),
                  pl.BlockSpec((1, 1, d), lambda i, be, na: (be[i], 0, 0))],
        out_specs=slab,
        scratch_shapes=[pltpu.VMEM((d, 2 * de), BF16), pltpu.VMEM((de, d), BF16)],
    )
    return pl.pallas_call(
        _experts_kernel,
        grid_spec=grid_spec,
        out_shape=jax.ShapeDtypeStruct(xs.shape, F32),
        compiler_params=_cparams("arbitrary"),
        name="moe_experts",
    )(blk_e, nact, xs, wgu, bgu, wd, bd)


GATHER_AHEAD = 2


def _final_kernel(*refs, ple_firsts, out_firsts, nblk, alpha):
    dest_refs, refs = refs[:GATHER_AHEAD + 1], refs[GATHER_AHEAD + 1:]
    x1_ref, prob_ref = refs[:2]
    refs = refs[2:]
    ple_refs, refs = refs[:len(ple_firsts)], refs[len(ple_firsts):]
    y_ref, wpp_ref, wpg_ref, bpg_ref, g2_ref, b2_ref = refs[:6]
    refs = refs[6:]
    out_refs, scratch = refs[:len(out_firsts)], refs[len(out_firsts):]
    ybufs, sems = scratch[:TOP_K], scratch[TOP_K]
    tm, d = out_refs[0].shape
    per = d // LANES
    i = pl.program_id(0)
    nbuf = GATHER_AHEAD + 1
    slot = lax.rem(i, nbuf)

    def gather(idx_ref, into):
        def group(g, carry):
            for u in range(DMA_ISSUE_UNROLL):
                t = g * DMA_ISSUE_UNROLL + u
                for k in range(TOP_K):
                    pltpu.make_async_copy(_slab_rows(y_ref, idx_ref[t * TOP_K + k], per),
                                          _slab_rows(ybufs[k], into * tm + t, per),
                                          sems.at[into]).start(priority=k % 2)
            return carry
        lax.fori_loop(0, tm // DMA_ISSUE_UNROLL, group, 0)

    @pl.when(i == 0)
    def _():
        for a in range(min(GATHER_AHEAD, nblk)):
            gather(dest_refs[a], a)

    @pl.when(i + GATHER_AHEAD < nblk)
    def _():
        gather(dest_refs[GATHER_AHEAD], lax.rem(i + GATHER_AHEAD, nbuf))

    half = pl.ds(pl.multiple_of(slot * (tm * per), tm * per), tm * per)
    for k in range(TOP_K):
        pltpu.make_async_copy(y_ref.at[pl.ds(0, tm * per)], ybufs[k].at[half], sems.at[slot]).wait()
    prob = prob_ref[...]
    moe = None
    for k in range(TOP_K):
        yk = _slab_load(ybufs[k].at[half], tm, d)
        moe = prob[:, k:k + 1] * yk if moe is None else moe + prob[:, k:k + 1] * yk
    r = alpha * _slab_load(x1_ref, tm, d) + moe
    gate = jax.nn.sigmoid(_dot(r.astype(BF16), wpg_ref[...]) + bpg_ref[...])
    ple = _select_part(i, ple_refs, ple_firsts)
    u = _dot(ple.astype(BF16), wpp_ref[...]) * gate
    res = _layer_norm(r + u, g2_ref[...], b2_ref[...])
    if len(out_refs) == 1:
        out_refs[0][...] = res
    else:
        bounds = list(out_firsts[1:]) + [pl.num_programs(0)]
        for ref, first, end in zip(out_refs, out_firsts, bounds):
            @pl.when((i >= first) & (i < end))
            def _(ref=ref):
                ref[...] = res


def _final(dest_flat, x1_slab, prob, ple_parts, layer, y_slab, wpp, wpg, bpg, g2, b2, out_rows, *, alpha):
    n = prob.shape[0]
    d = wpg.shape[0]
    per = d // LANES
    tm = FINAL_BLOCK
    nblk = n // tm
    row = lambda w: pl.BlockSpec((tm, w), lambda i: (i, 0))
    full = lambda a: pl.BlockSpec(a.shape, lambda i: (0,) * a.ndim)
    ple_specs, ple_firsts = _part_specs(ple_parts, tm, lead=layer)
    out_shapes = [jax.ShapeDtypeStruct((r, d), F32) for r in out_rows]
    out_specs, out_firsts = _part_specs(out_shapes, tm)
    return pl.pallas_call(
        functools.partial(_final_kernel, ple_firsts=ple_firsts, out_firsts=out_firsts, nblk=nblk, alpha=alpha),
        grid=(nblk,),
        in_specs=[pl.BlockSpec((tm * TOP_K,), lambda i, a=a: (jnp.minimum(i + a, nblk - 1),), memory_space=pltpu.SMEM)
                  for a in range(GATHER_AHEAD + 1)]
                 + [pl.BlockSpec((tm * per, LANES), lambda i: (i, 0)), row(prob.shape[1])]
                 + ple_specs
                 + [pl.BlockSpec(memory_space=pl.ANY), full(wpp), full(wpg), full(bpg), full(g2), full(b2)],
        out_specs=out_specs,
        out_shape=out_shapes,
        scratch_shapes=[pltpu.VMEM(((GATHER_AHEAD + 1) * tm * per, LANES), F32)] * TOP_K
                       + [pltpu.SemaphoreType.DMA((GATHER_AHEAD + 1,))],
        compiler_params=_cparams("arbitrary"),
        name="moe_combine_final",
    )(*([dest_flat] * (GATHER_AHEAD + 1)), x1_slab, prob, *ple_parts, y_slab, wpp, wpg, bpg, g2, b2)


def _routing_tables(top_e, pos, counts, n_blocks):
    c = MOE_ROWS
    e = jnp.arange(N_EXPERTS, dtype=jnp.int32)
    padded = (counts + c - 1) // c * c
    pad_end = jnp.sum(jnp.where(e[:, None] <= e[None, :], padded[:, None], 0), axis=0)
    pad_start = pad_end - padded
    start_of = jnp.sum(jnp.where(top_e[None] == e[:, None, None], pad_start[:, None, None], 0), axis=0)
    dest = (start_of + pos).T.reshape(-1)
    blk_first = jnp.arange(n_blocks, dtype=jnp.int32) * c
    blk_e = jnp.minimum(jnp.sum((pad_end[None, :] <= blk_first[:, None]).astype(jnp.int32), axis=1),
                        N_EXPERTS - 1)
    nact = pad_end[-1:] // c
    last_blk = jnp.where(padded > 0, pad_end // c - 1, -1)
    tail = nact + jnp.arange(n_blocks - top_e.size // c, dtype=jnp.int32)
    zero_blk = jnp.concatenate([last_blk, jnp.where(tail < n_blocks, tail, -1)])
    return dest.astype(jnp.int32), blk_e.astype(jnp.int32), nact.astype(jnp.int32), zero_blk.astype(jnp.int32)


def _layer(x_parts, ple_parts, lw, groups, out_rows, *, alpha, ln0):
    d = x_parts[0].shape[1]
    outs = _inproj(x_parts, ln0[0], ln0[1], lw['wg'], lw['wlr'], lw['wgk'], lw['bgk'], lw['wn'], apply_ln=ln0[2])
    if ln0[2]:
        x, outs = outs[0], outs[1:]
    else:
        (x,) = x_parts
    n = x.shape[0]
    qk, v, vt, gg, lg, qn, kn, vn = outs
    o_f, o_b = _gla(qk, v, vt, lg, groups)
    o_na = _na(qn, kn, vn, lw['na_bias'], groups)
    x1, top_e, prob, pos, cnt = _outproj(o_f, o_b, gg, o_na, x, lw['ng'], lw['wog'], lw['won'],
                                         lw['g1'], lw['b1'], lw['wrh'], lw['wrl'], lw['br'], alpha=alpha)
    n_blocks = (n * TOP_K + N_EXPERTS * (MOE_ROWS - 1) + MOE_ROWS - 1) // MOE_ROWS
    dest, blk_e, nact, zero_blk = _routing_tables(top_e[:TOP_K], pos[:TOP_K], cnt[:, 0], n_blocks)
    xs = _dispatch(zero_blk, dest, x1, n_blocks * MOE_ROWS, d // LANES)
    y = _experts(blk_e, nact, xs, lw['layer'], lw['wgu'], lw['bgu'], lw['wd'], lw['bd'])
    return _final(dest, x1, prob.T, ple_parts, lw['layer'], y, lw['wpp'], lw['wpg'], lw['bpg'], lw['g2'], lw['b2'],
                  out_rows, alpha=alpha)


def kernel(x_prompt, x_sample, p_prompt, p_sample, emb_ln_g, emb_ln_b, w_in, w_gk_f, b_gk_f, w_gk_b, b_gk_b, gla_norm_g, rpb, w_out, ln1_g, ln1_b, w_router, b_router, w_gu, b_gu, w_down, b_down, w_ple_proj, w_ple_gate, b_ple_gate, ln2_g, ln2_b):
    depth, d = w_in.shape[0], w_in.shape[1]
    alpha = float((2 * depth) ** 0.25)
    tb = TOKEN_BLOCK
    groups, first = [], 0
    for a in (x_prompt, x_sample):
        b, t, _ = a.shape
        assert t % tb == 0 and t % GRID_W == 0 and t // GRID_W >= NA_KH
        groups.append((first, t // tb))
        first += b * t // tb
    groups = tuple(groups)
    x_parts = [x_prompt.reshape(-1, d), x_sample.reshape(-1, d)]
    ple_parts = [p_prompt.reshape(depth, -1, p_prompt.shape[-1]), p_sample.reshape(depth, -1, p_sample.shape[-1])]
    n_all = sum(a.shape[0] for a in x_parts)
    row = lambda a: a.reshape(1, -1).astype(F32)
    c0, c1, c2 = 2 * GLA_QK + 2 * GLA_V, 2 * GLA_QK + 2 * GLA_V + 2 * GLA_RANK, w_in.shape[2]
    for i in range(depth):
        wgk = jnp.zeros((LANES, 2 * GLA_QK), F32)
        wgk = wgk.at[:GLA_RANK, :GLA_QK].set(w_gk_f[i]).at[GLA_RANK:2 * GLA_RANK, GLA_QK:].set(w_gk_b[i])
        wr = jnp.pad(w_router[i].astype(F32), ((0, 0), (0, LANES - N_EXPERTS)))
        lw = dict(
            wg=w_in[i, :, :c0].astype(BF16),
            wlr=jnp.pad(w_in[i, :, c0:c1], ((0, 0), (0, LANES - 2 * GLA_RANK))).astype(BF16),
            wgk=wgk.astype(BF16),
            bgk=jnp.concatenate([b_gk_f[i], b_gk_b[i]]).reshape(1, -1),
            wn=w_in[i, :, c1:c2].astype(BF16),
            na_bias=_na_bias_table(rpb[i]),
            ng=row(gla_norm_g[i]),
            wog=w_out[i, :GLA_V].astype(BF16), won=w_out[i, GLA_V:].astype(BF16),
            g1=row(ln1_g[i]), b1=row(ln1_b[i]),
            wrh=wr.astype(BF16), wrl=(wr - wr.astype(BF16).astype(F32)).astype(BF16),
            br=jnp.pad(b_router[i].astype(F32), (0, LANES - N_EXPERTS)).reshape(1, -1),
            layer=i, wgu=w_gu, bgu=b_gu[i][:, None, :], wd=w_down, bd=b_down[i][:, None, :],
            wpp=w_ple_proj[i].astype(BF16), wpg=w_ple_gate[i].astype(BF16), bpg=row(b_ple_gate[i]),
            g2=row(ln2_g[i]), b2=row(ln2_b[i]),
        )
        last = i == depth - 1
        out_rows = [a.shape[0] for a in x_parts] if last else [n_all]
        outs = _layer(x_parts if i == 0 else [x], ple_parts, lw, groups, out_rows,
                      alpha=alpha, ln0=(row(emb_ln_g), row(emb_ln_b), i == 0))
        if not last:
            (x,) = outs
    return (outs[0].reshape(x_prompt.shape), outs[1].reshape(x_sample.shape))
```

```python
import functools

import numpy as np
import jax
import jax.numpy as jnp
from jax import lax
from jax.experimental import pallas as pl
from jax.experimental.pallas import tpu as pltpu

GRID_W = 64
GLA_HEADS, GLA_DK, GLA_DV, GLA_RANK = 4, 64, 128, 16
GLA_NORMALIZER = 16.0
NA_HEADS, NA_DH, NA_KH, NA_KW = 8, 64, 8, 16
N_EXPERTS, TOP_K = 32, 4
SWIGLU_ALPHA, SWIGLU_LIMIT = 1.702, 7.0
LN_EPS, RMS_EPS = 1e-5, 1e-6

GLA_QK = GLA_HEADS * GLA_DK
GLA_V = GLA_HEADS * GLA_DV
NA_W = NA_HEADS * NA_DH

LANES = 128
V7X_VMEM_LIMIT_BYTES = 56 * 1024 * 1024

TOKEN_BLOCK = 512
GLA_CHUNK = 128
GLA_SAFE_LOG_DECAY = 40.0
NA_ROWS_PER_BLOCK = TOKEN_BLOCK // GRID_W
NA_SCORE_LEAD = 4
MOE_ROWS = 1024
FINAL_BLOCK = 256
NEG_BIG = -1e30

F32 = jnp.float32
BF16 = jnp.bfloat16


def _cparams(*sem):
    return pltpu.CompilerParams(dimension_semantics=sem, vmem_limit_bytes=V7X_VMEM_LIMIT_BYTES)


def _dot(a, b, precision=None):
    return jnp.dot(a, b, preferred_element_type=F32, precision=precision)


def _dot_nt(a, b, precision=None):
    return lax.dot_general(a, b, (((1,), (1,)), ((), ())), preferred_element_type=F32, precision=precision)


def _dot_tn(a, b):
    return lax.dot_general(a, b, (((0,), (0,)), ((), ())), preferred_element_type=F32)


def _layer_norm(x, g, b):
    mu = jnp.mean(x, axis=-1, keepdims=True)
    xc = x - mu
    var = jnp.mean(xc * xc, axis=-1, keepdims=True)
    return xc * lax.rsqrt(var + LN_EPS) * g + b


def _slab_load(ref, n, width):
    per = width // LANES
    return jnp.concatenate([ref[pl.ds(s, n, stride=per), :] for s in range(per)], axis=-1)


def _slab_store(ref, x):
    n, width = x.shape
    per = width // LANES
    for s in range(per):
        ref[pl.ds(s, n, stride=per), :] = x[:, s * LANES:(s + 1) * LANES]


def _seq_local(blk, groups):
    loc = lax.rem(blk - groups[-1][0], groups[-1][1])
    bps = jnp.int32(groups[-1][1])
    for (first, per), nxt in zip(reversed(groups[:-1]), reversed(groups[1:])):
        inside = blk < nxt[0]
        loc = jnp.where(inside, lax.rem(blk - first, per), loc)
        bps = jnp.where(inside, per, bps)
    return loc, bps


def _part_specs(parts, block_rows, lead=None):
    specs, firsts, first = [], [], 0
    for a in parts:
        rows, width = a.shape[-2], a.shape[-1]
        nb = rows // block_rows
        if lead is None:
            spec = pl.BlockSpec((block_rows, width), lambda i, first=first, nb=nb: (jnp.clip(i - first, 0, nb - 1), 0))
        else:
            spec = pl.BlockSpec((None, block_rows, width),
                                lambda i, first=first, nb=nb: (lead, jnp.clip(i - first, 0, nb - 1), 0))
        specs.append(spec)
        firsts.append(first)
        first += nb
    return specs, tuple(firsts)


def _select_part(i, refs, firsts):
    x = refs[0][...]
    for ref, first in zip(refs[1:], firsts[1:]):
        x = jnp.where(i >= first, ref[...], x)
    return x


def _inproj_kernel(*refs, firsts, apply_ln):
    x_refs, refs = refs[:len(firsts)], refs[len(firsts):]
    g_ref, b_ref, wg_ref, wlr_ref, wgk_ref, bgk_ref, wn_ref = refs[:7]
    out_refs = refs[7:]
    if apply_ln:
        x0_ref, qk_ref, v_ref, vt_ref, gg_ref, lg_ref, qn_ref, kn_ref, vn_ref = out_refs
    else:
        qk_ref, v_ref, vt_ref, gg_ref, lg_ref, qn_ref, kn_ref, vn_ref = out_refs
    x = _select_part(pl.program_id(0), x_refs, firsts)
    if apply_ln:
        x = _layer_norm(x, g_ref[...], b_ref[...])
        x0_ref[...] = x
    xb = x.astype(BF16)
    zg = _dot(xb, wg_ref[...])
    qk_ref[:, :GLA_QK] = (zg[:, :GLA_QK] * GLA_DK ** -0.5).astype(BF16)
    qk_ref[:, GLA_QK:] = zg[:, GLA_QK:2 * GLA_QK].astype(BF16)
    v = zg[:, 2 * GLA_QK:2 * GLA_QK + GLA_V]
    v_ref[...] = v.astype(BF16)
    vt_ref[...] = v.T.astype(BF16)
    gg_ref[...] = zg[:, 2 * GLA_QK + GLA_V:]
    lr = _dot(xb, wlr_ref[...])
    pre = _dot(lr.astype(BF16), wgk_ref[...]) + bgk_ref[...]
    log_sig = jnp.minimum(pre, 0.0) - jnp.log1p(jnp.exp(-jnp.abs(pre)))
    lg_ref[...] = log_sig * (1.0 / GLA_NORMALIZER)
    zn = _dot(xb, wn_ref[...])
    qn_ref[...] = (zn[:, :NA_W] * NA_DH ** -0.5).astype(BF16)
    kn_ref[...] = zn[:, NA_W:2 * NA_W].astype(BF16)
    vn_ref[...] = zn[:, 2 * NA_W:].astype(BF16)


def _inproj(x_parts, ln_g, ln_b, wg, wlr, wgk, bgk, wn, *, apply_ln):
    n = sum(a.shape[0] for a in x_parts)
    d = x_parts[0].shape[1]
    tm = TOKEN_BLOCK
    row = lambda w: pl.BlockSpec((tm, w), lambda i: (i, 0))
    full = lambda a: pl.BlockSpec(a.shape, lambda i: (0,) * a.ndim)
    x_specs, firsts = _part_specs(x_parts, tm)
    outs = [(GLA_QK * 2, BF16, False), (GLA_V, BF16, False), (GLA_V, BF16, True), (GLA_V, F32, False),
            (2 * GLA_QK, F32, False), (NA_W, BF16, False), (NA_W, BF16, False), (NA_W, BF16, False)]
    if apply_ln:
        outs = [(d, F32, False)] + outs
    return pl.pallas_call(
        functools.partial(_inproj_kernel, firsts=firsts, apply_ln=apply_ln),
        grid=(n // tm,),
        in_specs=x_specs + [full(ln_g), full(ln_b), full(wg), full(wlr), full(wgk), full(bgk), full(wn)],
        out_specs=[pl.BlockSpec((w, tm), lambda i: (0, i)) if tr else row(w) for w, _, tr in outs],
        out_shape=[jax.ShapeDtypeStruct((w, n) if tr else (n, w), t) for w, t, tr in outs],
        compiler_params=_cparams("arbitrary"),
        name="inproj_ln" if apply_ln else "inproj",
    )(*x_parts, ln_g, ln_b, wg, wlr, wgk, bgk, wn)


GLA_HEADS_PER_TILE = LANES // GLA_DK
assert GLA_DV == LANES and GLA_HEADS % GLA_HEADS_PER_TILE == 0


def _head_lane_masks(rows):
    lane = lax.broadcasted_iota(jnp.int32, (rows, LANES), 1)
    return [(lane >= hh * GLA_DK) & (lane < (hh + 1) * GLA_DK) for hh in range(GLA_HEADS_PER_TILE)]


def _gla_chunk_fast(qk_ref, v_ref, vt_ref, lg_ref, o_ref, s_ref, d, c, fwd):
    C = GLA_CHUNK
    rows = pl.ds(c * C, C)
    r_i = lax.broadcasted_iota(jnp.int32, (C, C), 0)
    c_i = lax.broadcasted_iota(jnp.int32, (C, C), 1)
    tri = ((r_i >= c_i) if fwd else (r_i <= c_i)).astype(BF16)
    lg = lg_ref[rows, :]
    lg1 = lg.astype(BF16)
    rem = lg - lg1.astype(F32)
    lg2 = rem.astype(BF16)
    lg3 = (rem - lg2.astype(F32)).astype(BF16)
    cum = _dot(tri, lg1) + (_dot(tri, lg2) + _dot(tri, lg3))
    yield
    tot = cum[C - 1:C, :] if fwd else cum[0:1, :]
    q = qk_ref[rows, :GLA_QK].astype(F32)
    k = qk_ref[rows, GLA_QK:].astype(F32)
    qt = (q * jnp.exp(cum)).astype(BF16)
    kt = (k * jnp.exp(-cum)).astype(BF16)
    kd = (k * jnp.exp(tot - cum)).astype(BF16)
    dec = jnp.exp(tot)
    mask = (r_i >= c_i) if fwd else (r_i < c_i)
    in_head = _head_lane_masks(C)
    zero = jnp.zeros((C, LANES), BF16)
    tiles = [slice(p * LANES, (p + 1) * LANES) for p in range(GLA_HEADS // GLA_HEADS_PER_TILE)]
    yield
    states, aos = [], []
    for p, tile in enumerate(tiles):
        s = s_ref[d, p]
        q_heads = jnp.concatenate([jnp.where(m, qt[:, tile], zero) for m in in_head], axis=0)
        aos.append(_dot_nt(q_heads, jnp.concatenate([kt[:, tile], s.astype(BF16)], axis=0)))
        states.append(s)
        yield
    for p, tile in enumerate(tiles):
        vt = [vt_ref[(p * GLA_HEADS_PER_TILE + hh) * GLA_DV:(p * GLA_HEADS_PER_TILE + hh + 1) * GLA_DV,
                     c * C:(c + 1) * C] for hh in range(GLA_HEADS_PER_TILE)]
        kd_heads = jnp.concatenate([jnp.where(m, kd[:, tile], zero) for m in in_head], axis=0)
        s_ref[d, p] = dec[:, tile] * states[p] + _dot(jnp.concatenate(vt, axis=1), kd_heads)
        yield
    for p in range(len(tiles)):
        for hh in range(GLA_HEADS_PER_TILE):
            h = p * GLA_HEADS_PER_TILE + hh
            vs = slice(h * GLA_DV, (h + 1) * GLA_DV)
            a = jnp.where(mask, aos[p][hh * C:(hh + 1) * C, :C], 0.0).astype(BF16)
            o_ref[rows, vs] = _dot(a, v_ref[rows, vs]) + aos[p][hh * C:(hh + 1) * C, C:]
            yield


GLA_STAGES_TO_STATE = 2 + 2 * (GLA_HEADS // GLA_HEADS_PER_TILE)


def _interleave(staggered):
    pending = sorted(staggered, key=lambda item: item[0])
    live, tick = [], 0
    while pending or live:
        while pending and pending[0][0] <= tick:
            live.append(pending.pop(0)[1])
        for g in list(live):
            try:
                next(g)
            except StopIteration:
                live.remove(g)
        tick += 1


def _gla_block_slow(qk_ref, v_ref, lg_ref, o_ref, s_ref, d, fwd, tb):
    G = 16

    def group(n, carry):
        rows = pl.ds(pl.multiple_of((n if fwd else tb // G - 1 - n) * G, G), G)
        gate = jnp.exp(lg_ref[rows, :])
        q = qk_ref[rows, :GLA_QK].astype(F32)
        k = qk_ref[rows, GLA_QK:]
        qs = (q if fwd else q * gate).astype(BF16)
        row = lax.broadcasted_iota(jnp.int32, (G, LANES), 0)
        in_head = _head_lane_masks(G)
        zero = jnp.zeros((G, LANES), BF16)
        for p in range(GLA_HEADS // GLA_HEADS_PER_TILE):
            tile = slice(p * LANES, (p + 1) * LANES)
            heads = [p * GLA_HEADS_PER_TILE + hh for hh in range(GLA_HEADS_PER_TILE)]
            vs = [slice(h * GLA_DV, (h + 1) * GLA_DV) for h in heads]
            v = [v_ref[rows, sl] for sl in vs]
            k_heads = [jnp.where(m, k[:, tile], zero) for m in in_head]
            s = s_ref[d, p]
            o = [jnp.zeros((G, GLA_DV), F32) for _ in heads]
            for r in (range(G) if fwd else reversed(range(G))):
                upd = None
                for hh in range(len(heads)):
                    term = _dot_tn(jnp.where(row == r, v[hh], zero), k_heads[hh])
                    upd = term if upd is None else upd + term
                s_new = gate[r:r + 1, tile] * s + upd
                s_read = (s_new if fwd else s).astype(BF16)
                for hh in range(len(heads)):
                    o[hh] = o[hh] + _dot_nt(jnp.where((row == r) & in_head[hh], qs[:, tile], zero), s_read)
                s = s_new
            for hh in range(len(heads)):
                o_ref[rows, vs[hh]] = o[hh]
            s_ref[d, p] = s
        return carry

    lax.fori_loop(0, tb // G, group, 0)


def _gla_kernel(qkf_ref, vf_ref, vtf_ref, lgf_ref, qkb_ref, vb_ref, vtb_ref, lgb_ref, of_ref, ob_ref, s_ref,
                *, groups, tb):
    i = pl.program_id(0)
    nblk = pl.num_programs(0)
    loc_f, _ = _seq_local(i, groups)
    loc_b, bps_b = _seq_local(nblk - 1 - i, groups)

    @pl.when(loc_f == 0)
    def _():
        s_ref[0] = jnp.zeros(s_ref.shape[1:], F32)

    @pl.when(loc_b == bps_b - 1)
    def _():
        s_ref[1] = jnp.zeros(s_ref.shape[1:], F32)

    nc = tb // GLA_CHUNK
    worst = jnp.float32(0.0)
    for ref in (lgf_ref, lgb_ref):
        for c in range(nc):
            tot = jnp.sum(ref[pl.ds(c * GLA_CHUNK, GLA_CHUNK), :], axis=0, keepdims=True)
            worst = jnp.minimum(worst, jnp.min(tot))
    safe = worst > -GLA_SAFE_LOG_DECAY

    @pl.when(safe)
    def _():
        work = []
        for c in range(nc):
            work.append((c * GLA_STAGES_TO_STATE,
                         _gla_chunk_fast(qkf_ref, vf_ref, vtf_ref, lgf_ref, of_ref, s_ref, 0, c, True)))
            work.append((c * GLA_STAGES_TO_STATE,
                         _gla_chunk_fast(qkb_ref, vb_ref, vtb_ref, lgb_ref, ob_ref, s_ref, 1, nc - 1 - c, False)))
        _interleave(work)

    @pl.when(jnp.logical_not(safe))
    def _():
        _gla_block_slow(qkf_ref, vf_ref, lgf_ref, of_ref, s_ref, 0, True, tb)
        _gla_block_slow(qkb_ref, vb_ref, lgb_ref, ob_ref, s_ref, 1, False, tb)


def _gla(qk, v, vt, lg, groups):
    n = qk.shape[0]
    tb = TOKEN_BLOCK
    nblk = n // tb
    fwd = lambda w, j: pl.BlockSpec((tb, w), lambda i: (i, j))
    bwd = lambda w, j: pl.BlockSpec((tb, w), lambda i: (nblk - 1 - i, j))
    return pl.pallas_call(
        functools.partial(_gla_kernel, groups=groups, tb=tb),
        grid=(nblk,),
        in_specs=[fwd(2 * GLA_QK, 0), fwd(GLA_V, 0), pl.BlockSpec((GLA_V, tb), lambda i: (0, i)), fwd(GLA_QK, 0),
                  bwd(2 * GLA_QK, 0), bwd(GLA_V, 0), pl.BlockSpec((GLA_V, tb), lambda i: (0, nblk - 1 - i)),
                  bwd(GLA_QK, 1)],
        out_specs=[fwd(GLA_V, 0), bwd(GLA_V, 0)],
        out_shape=[jax.ShapeDtypeStruct((n, GLA_V), F32)] * 2,
        scratch_shapes=[pltpu.VMEM((2, GLA_HEADS // GLA_HEADS_PER_TILE, GLA_DV, LANES), F32)],
        compiler_params=_cparams("arbitrary"),
        name="gla",
    )(qk, v, vt, lg, qk, v, vt, lg)


def _na_bias_table(rpb):
    w = GRID_W
    col = np.arange(w)
    start = np.clip(col - NA_KW // 2, 0, w - NA_KW)
    kc = np.arange(w)
    in_win = (kc[None, :] >= start[:, None]) & (kc[None, :] < start[:, None] + NA_KW)
    t = jnp.stack([rpb[:, NA_KH - 1 - s:2 * NA_KH - 1 - s] for s in range(NA_KH)], axis=1)
    t = jnp.pad(t, ((0, 0), (0, 0), (0, 0), (w, w)))
    t = jnp.stack([t[..., w + NA_KW - 1 - c:2 * w + NA_KW - 1 - c] for c in range(w)], axis=3)
    t = jnp.where(jnp.asarray(in_win)[None, None, None], t, NEG_BIG)
    t = t.transpose(1, 0, 3, 2, 4)
    heads = LANES // NA_DH
    return t.reshape(NA_KH, NA_HEADS // heads, heads * w, NA_KH * w).astype(F32)


def _na_window_start(blk, tb, n_tokens):
    return jnp.clip((blk - 1) * tb, 0, n_tokens - 3 * tb)


def _na_kernel(q_ref, kwin_ref, vwin_ref, bias_ref, o_ref, *, groups, n_tokens):
    tb = TOKEN_BLOCK
    w = GRID_W
    rpb_rows = NA_ROWS_PER_BLOCK
    blk = pl.program_id(1)
    loc, bps = _seq_local(blk, groups)
    rows_in_seq = bps * rpb_rows
    seq_first = (blk - loc) * tb - _na_window_start(blk, tb, n_tokens)
    lane = lax.broadcasted_iota(jnp.int32, (w, LANES), 1)
    heads = LANES // NA_DH
    in_head = [(lane >= hh * NA_DH) & (lane < (hh + 1) * NA_DH) for hh in range(heads)]
    win = NA_KH * w

    def probs(j):
        r = loc * rpb_rows + j
        rs = jnp.clip(r - NA_KH // 2, 0, rows_in_seq - NA_KH)
        off = pl.multiple_of(seq_first + rs * w, w)
        q = q_ref[pl.ds(j * w, w), :]
        q_heads = jnp.concatenate([jnp.where(m, q, jnp.zeros_like(q)) for m in in_head], axis=0)
        s = _dot_nt(q_heads, kwin_ref[pl.ds(off, win), :]) + bias_ref[r - rs, 0]
        p = jnp.exp(s - jnp.max(s, axis=-1, keepdims=True))
        return j, off, p.astype(BF16), jnp.sum(p, axis=-1, keepdims=True)

    def output(j, off, p, l):
        o = _dot(p, vwin_ref[pl.ds(off, win), :]) * (1.0 / l)
        out = o[:w]
        for hh in range(1, heads):
            out = jnp.where(in_head[hh], o[hh * w:(hh + 1) * w], out)
        o_ref[pl.ds(j * w, w), :] = out.astype(o_ref.dtype)

    pending = []
    for j in range(rpb_rows):
        pending.append(probs(j))
        if len(pending) > NA_SCORE_LEAD:
            output(*pending.pop(0))
    for item in pending:
        output(*item)


def _na(q, k, v, bias, groups):
    n = q.shape[0]
    tb = TOKEN_BLOCK
    nblk = n // tb
    nhp = NA_W // LANES
    cur = pl.BlockSpec((tb, LANES), lambda hp, i: (i, hp))
    window = pl.BlockSpec((pl.Element(3 * tb), pl.Element(LANES)),
                          lambda hp, i: (pl.multiple_of(_na_window_start(i, tb, n), tb),
                                         pl.multiple_of(hp * LANES, LANES)))
    return pl.pallas_call(
        functools.partial(_na_kernel, groups=groups, n_tokens=n),
        grid=(nhp, nblk),
        in_specs=[cur, window, window,
                  pl.BlockSpec((NA_KH, 1) + bias.shape[2:], lambda hp, i: (0, hp, 0, 0))],
        out_specs=cur,
        out_shape=jax.ShapeDtypeStruct((n, NA_W), BF16),
        compiler_params=_cparams("parallel", "parallel"),
        name="natten",
    )(q, k, v, bias)


OUTPROJ_SUB_BLOCKS = 2
OUTPROJ_STAGGER = 0


def _outproj_kernel(of_ref, ob_ref, gg_ref, ona_ref, x_ref, ng_ref, wog_ref, won_ref, g1_ref, b1_ref,
                    wrh_ref, wrl_ref, br_ref, x1_ref, e_ref, p_ref, pos_ref, cnt_ref, run_ref, ut_ref,
                    *, alpha):
    i = pl.program_id(0)
    tm, d = x_ref.shape
    th = tm // OUTPROJ_SUB_BLOCKS
    per = d // LANES

    @pl.when(i == 0)
    def _():
        run_ref[...] = jnp.zeros(run_ref.shape, F32)
        r_i = lax.broadcasted_iota(jnp.int32, (th, th), 0)
        c_i = lax.broadcasted_iota(jnp.int32, (th, th), 1)
        ut_ref[...] = (r_i < c_i).astype(BF16)

    runs = [run_ref[...]]

    def sub_block(j):
        rows = pl.ds(j * th, th)
        o = of_ref[rows, :] + ob_ref[rows, :]
        parts = []
        for h in range(GLA_HEADS):
            oh = o[:, h * GLA_DV:(h + 1) * GLA_DV]
            parts.append(oh * lax.rsqrt(jnp.mean(oh * oh, axis=-1, keepdims=True) + RMS_EPS))
        g = gg_ref[rows, :]
        o = jnp.concatenate(parts, axis=-1) * ng_ref[...] * (g * jax.nn.sigmoid(g))
        yield
        mix = _dot(o.astype(BF16), wog_ref[...]) + _dot(ona_ref[rows, :], won_ref[...])
        yield
        x1 = _layer_norm(alpha * x_ref[rows, :] + mix, g1_ref[...], b1_ref[...])
        _slab_store(x1_ref.at[pl.ds(j * th * per, th * per)], x1)
        x_hi = x1.astype(BF16)
        x_lo = (x1 - x_hi.astype(F32)).astype(BF16)
        yield
        logits = (_dot(x_hi, wrh_ref[...]) + (_dot(x_hi, wrl_ref[...]) + _dot(x_lo, wrh_ref[...])) + br_ref[...])
        yield
        lg = logits.T[:N_EXPERTS]
        eid = lax.broadcasted_iota(jnp.int32, (N_EXPERTS, th), 0)
        out_row = lax.broadcasted_iota(jnp.int32, (8, th), 0)
        chosen = jnp.zeros((N_EXPERTS, th), F32)
        e_out = jnp.zeros((8, th), jnp.int32)
        vals, sels = [], []
        for k in range(TOP_K):
            m = jnp.max(lg, axis=0, keepdims=True)
            idx = jnp.min(jnp.where(lg == m, eid, N_EXPERTS), axis=0, keepdims=True)
            sel = eid == idx
            sels.append(sel)
            vals.append(m)
            e_out = jnp.where(out_row == k, idx, e_out)
            chosen = jnp.where(sel, 1.0, chosen)
            lg = jnp.where(sel, -jnp.inf, lg)
        ex = [jnp.exp(v - vals[0]) for v in vals]
        inv = 1.0 / functools.reduce(lambda a, b: a + b, ex)
        p_out = jnp.zeros((8, th), F32)
        for k in range(TOP_K):
            p_out = jnp.where(out_row == k, ex[k] * inv, p_out)
        p_ref[:, rows] = p_out
        e_ref[:, rows] = e_out
        counts = jnp.sum(chosen, axis=1, keepdims=True)
        yield
        while len(runs) <= j:
            yield
        before = _dot(chosen.astype(BF16), ut_ref[...]) + runs[j][:, :1]
        runs.append(runs[j] + counts)
        pos = jnp.zeros((8, th), jnp.int32)
        for k in range(TOP_K):
            pk = jnp.sum(jnp.where(sels[k], before, 0.0), axis=0, keepdims=True)
            pos = jnp.where(out_row == k, pk.astype(jnp.int32), pos)
        pos_ref[:, rows] = pos

    _interleave([(j * OUTPROJ_STAGGER, sub_block(j)) for j in range(OUTPROJ_SUB_BLOCKS)])
    run_ref[...] = runs[-1]
    cnt_ref[...] = runs[-1].astype(jnp.int32)


def _outproj(o_f, o_b, gg, o_na, x, ng, wog, won, g1, b1, wrh, wrl, br, *, alpha):
    n, d = x.shape
    tm = TOKEN_BLOCK
    row = lambda w: pl.BlockSpec((tm, w), lambda i: (i, 0))
    col = pl.BlockSpec((8, tm), lambda i: (0, i))
    full = lambda a: pl.BlockSpec(a.shape, lambda i: (0,) * a.ndim)
    return pl.pallas_call(
        functools.partial(_outproj_kernel, alpha=alpha),
        grid=(n // tm,),
        in_specs=[row(GLA_V), row(GLA_V), row(GLA_V), row(NA_W), row(d), full(ng), full(wog), full(won),
                  full(g1), full(b1), full(wrh), full(wrl), full(br)],
        out_specs=[pl.BlockSpec((tm * d // LANES, LANES), lambda i: (i, 0)), col, col, col,
                   pl.BlockSpec((N_EXPERTS, LANES), lambda i: (0, 0))],
        out_shape=[jax.ShapeDtypeStruct((n * d // LANES, LANES), F32),
                   jax.ShapeDtypeStruct((8, n), jnp.int32), jax.ShapeDtypeStruct((8, n), F32),
                   jax.ShapeDtypeStruct((8, n), jnp.int32), jax.ShapeDtypeStruct((N_EXPERTS, LANES), jnp.int32)],
        scratch_shapes=[pltpu.VMEM((N_EXPERTS, LANES), F32),
                        pltpu.VMEM((tm // OUTPROJ_SUB_BLOCKS, tm // OUTPROJ_SUB_BLOCKS), BF16)],
        compiler_params=_cparams("arbitrary"),
        name="outproj_router",
    )(o_f, o_b, gg, o_na, x, ng, wog, won, g1, b1, wrh, wrl, br)


def _slab_rows(ref, row, per):
    return ref.at[pl.ds(pl.multiple_of(row * per, per), per)]


DMA_ISSUE_UNROLL = 8


def _dispatch_kernel(zero_blk_ref, dest_ref, x_ref, xs_ref, zbuf, sem, *, per):
    tm = x_ref.shape[0] // per
    c = MOE_ROWS

    @pl.when(pl.program_id(0) == 0)
    def _():
        zbuf[...] = jnp.zeros(zbuf.shape, zbuf.dtype)

        def fill(j):
            return pltpu.make_async_copy(
                zbuf, xs_ref.at[pl.ds(pl.multiple_of(zero_blk_ref[j] * (c * per), c * per), c * per)], sem)

        def start(j, carry):
            @pl.when(zero_blk_ref[j] >= 0)
            def _():
                fill(j).start()
            return carry

        def wait(j, carry):
            @pl.when(zero_blk_ref[j] >= 0)
            def _():
                fill(j).wait()
            return carry

        lax.fori_loop(0, zero_blk_ref.shape[0], start, 0)
        lax.fori_loop(0, zero_blk_ref.shape[0], wait, 0)

    def group(g, carry):
        for u in range(DMA_ISSUE_UNROLL):
            t = g * DMA_ISSUE_UNROLL + u
            for k in range(TOP_K):
                pltpu.make_async_copy(_slab_rows(x_ref, t, per), _slab_rows(xs_ref, dest_ref[t * TOP_K + k], per),
                                      sem).start(priority=k % 2)
        return carry

    lax.fori_loop(0, tm // DMA_ISSUE_UNROLL, group, 0)
    for k in range(TOP_K):
        pltpu.make_async_copy(x_ref, xs_ref.at[pl.ds(0, tm * per)], sem).wait()


def _dispatch(zero_blk, dest_flat, x1_slab, n_slots, per):
    n = x1_slab.shape[0] // per
    tm = TOKEN_BLOCK
    grid_spec = pltpu.PrefetchScalarGridSpec(
        num_scalar_prefetch=1,
        grid=(n // tm,),
        in_specs=[pl.BlockSpec((tm * TOP_K,), lambda i, zb: (i,), memory_space=pltpu.SMEM),
                  pl.BlockSpec((tm * per, LANES), lambda i, zb: (i, 0))],
        out_specs=pl.BlockSpec(memory_space=pl.ANY),
        scratch_shapes=[pltpu.VMEM((MOE_ROWS * per, LANES), x1_slab.dtype), pltpu.SemaphoreType.DMA(())],
    )
    return pl.pallas_call(
        functools.partial(_dispatch_kernel, per=per),
        grid_spec=grid_spec,
        out_shape=jax.ShapeDtypeStruct((n_slots * per, LANES), x1_slab.dtype),
        compiler_params=_cparams("arbitrary"),
        name="moe_dispatch",
    )(zero_blk, dest_flat, x1_slab)


def _experts_kernel(be_ref, nact_ref, xs_ref, wgu_ref, bgu_ref, wd_ref, bd_ref, y_ref, wgu_bf, wd_bf):
    i = pl.program_id(0)
    d, de = wd_ref.shape[2], wd_ref.shape[1]
    c = MOE_ROWS

    @pl.when((i == 0) | (be_ref[i] != be_ref[jnp.maximum(i - 1, 0)]))
    def _():
        wgu_bf[...] = wgu_ref[0].astype(BF16)
        wd_bf[...] = wd_ref[0].astype(BF16)

    @pl.when(i < nact_ref[0])
    def _():
        x = _slab_load(xs_ref, c, d).astype(BF16)
        h = _dot(x, wgu_bf[...]) + bgu_ref[0]
        gate = jnp.minimum(h[:, :de], SWIGLU_LIMIT)
        up = jnp.clip(h[:, de:], -SWIGLU_LIMIT, SWIGLU_LIMIT)
        glu = gate * jax.nn.sigmoid(gate * SWIGLU_ALPHA)
        act = ((up + 1.0) * glu).astype(BF16)
        _slab_store(y_ref, _dot(act, wd_bf[...]) + bd_ref[0])

    @pl.when(i >= nact_ref[0])
    def _():
        y_ref[...] = jnp.zeros(y_ref.shape, y_ref.dtype)


def _experts(blk_e, nact, xs, layer, wgu, bgu, wd, bd):
    d, de = wd.shape[3], wd.shape[2]
    per = d // LANES
    c = MOE_ROWS
    slab = pl.BlockSpec((c * per, LANES), lambda i, be, na: (i, 0))
    grid_spec = pltpu.PrefetchScalarGridSpec(
        num_scalar_prefetch=2,
        grid=(xs.shape[0] // (c * per),),
        in_specs=[slab,
                  pl.BlockSpec((None, 1, d, 2 * de), lambda i, be, na: (layer, be[i], 0, 0)),
                  pl.BlockSpec((1, 1, 2 * de), lambda i, be, na: (be[i], 0, 0)),
                  pl.BlockSpec((None, 1, de, d), lambda i, be, na: (layer, be[i], 0, 0)),
                  pl.BlockSpec((1, 1, d), lambda i, be, na: (be[i], 0, 0))],
        out_specs=slab,
        scratch_shapes=[pltpu.VMEM((d, 2 * de), BF16), pltpu.VMEM((de, d), BF16)],
    )
    return pl.pallas_call(
        _experts_kernel,
        grid_spec=grid_spec,
        out_shape=jax.ShapeDtypeStruct(xs.shape, F32),
        compiler_params=_cparams("arbitrary"),
        name="moe_experts",
    )(blk_e, nact, xs, wgu, bgu, wd, bd)


GATHER_AHEAD = 2


def _final_kernel(*refs, ple_firsts, out_firsts, nblk, alpha):
    dest_refs, refs = refs[:GATHER_AHEAD + 1], refs[GATHER_AHEAD + 1:]
    x1_ref, prob_ref = refs[:2]
    refs = refs[2:]
    ple_refs, refs = refs[:len(ple_firsts)], refs[len(ple_firsts):]
    y_ref, wpp_ref, wpg_ref, bpg_ref, g2_ref, b2_ref = refs[:6]
    refs = refs[6:]
    out_refs, scratch = refs[:len(out_firsts)], refs[len(out_firsts):]
    ybufs, sems = scratch[:TOP_K], scratch[TOP_K]
    tm, d = out_refs[0].shape
    per = d // LANES
    i = pl.program_id(0)
    nbuf = GATHER_AHEAD + 1
    slot = lax.rem(i, nbuf)

    def gather(idx_ref, into):
        def group(g, carry):
            for u in range(DMA_ISSUE_UNROLL):
                t = g * DMA_ISSUE_UNROLL + u
                for k in range(TOP_K):
                    pltpu.make_async_copy(_slab_rows(y_ref, idx_ref[t * TOP_K + k], per),
                                          _slab_rows(ybufs[k], into * tm + t, per),
                                          sems.at[into]).start(priority=k % 2)
            return carry
        lax.fori_loop(0, tm // DMA_ISSUE_UNROLL, group, 0)

    @pl.when(i == 0)
    def _():
        for a in range(min(GATHER_AHEAD, nblk)):
            gather(dest_refs[a], a)

    @pl.when(i + GATHER_AHEAD < nblk)
    def _():
        gather(dest_refs[GATHER_AHEAD], lax.rem(i + GATHER_AHEAD, nbuf))

    half = pl.ds(pl.multiple_of(slot * (tm * per), tm * per), tm * per)
    for k in range(TOP_K):
        pltpu.make_async_copy(y_ref.at[pl.ds(0, tm * per)], ybufs[k].at[half], sems.at[slot]).wait()
    prob = prob_ref[...]
    moe = None
    for k in range(TOP_K):
        yk = _slab_load(ybufs[k].at[half], tm, d)
        moe = prob[:, k:k + 1] * yk if moe is None else moe + prob[:, k:k + 1] * yk
    r = alpha * _slab_load(x1_ref, tm, d) + moe
    gate = jax.nn.sigmoid(_dot(r.astype(BF16), wpg_ref[...]) + bpg_ref[...])
    ple = _select_part(i, ple_refs, ple_firsts)
    u = _dot(ple.astype(BF16), wpp_ref[...]) * gate
    res = _layer_norm(r + u, g2_ref[...], b2_ref[...])
    if len(out_refs) == 1:
        out_refs[0][...] = res
    else:
        bounds = list(out_firsts[1:]) + [pl.num_programs(0)]
        for ref, first, end in zip(out_refs, out_firsts, bounds):
            @pl.when((i >= first) & (i < end))
            def _(ref=ref):
                ref[...] = res


def _final(dest_flat, x1_slab, prob, ple_parts, layer, y_slab, wpp, wpg, bpg, g2, b2, out_rows, *, alpha):
    n = prob.shape[0]
    d = wpg.shape[0]
    per = d // LANES
    tm = FINAL_BLOCK
    nblk = n // tm
    row = lambda w: pl.BlockSpec((tm, w), lambda i: (i, 0))
    full = lambda a: pl.BlockSpec(a.shape, lambda i: (0,) * a.ndim)
    ple_specs, ple_firsts = _part_specs(ple_parts, tm, lead=layer)
    out_shapes = [jax.ShapeDtypeStruct((r, d), F32) for r in out_rows]
    out_specs, out_firsts = _part_specs(out_shapes, tm)
    return pl.pallas_call(
        functools.partial(_final_kernel, ple_firsts=ple_firsts, out_firsts=out_firsts, nblk=nblk, alpha=alpha),
        grid=(nblk,),
        in_specs=[pl.BlockSpec((tm * TOP_K,), lambda i, a=a: (jnp.minimum(i + a, nblk - 1),), memory_space=pltpu.SMEM)
                  for a in range(GATHER_AHEAD + 1)]
                 + [pl.BlockSpec((tm * per, LANES), lambda i: (i, 0)), row(prob.shape[1])]
                 + ple_specs
                 + [pl.BlockSpec(memory_space=pl.ANY), full(wpp), full(wpg), full(bpg), full(g2), full(b2)],
        out_specs=out_specs,
        out_shape=out_shapes,
        scratch_shapes=[pltpu.VMEM(((GATHER_AHEAD + 1) * tm * per, LANES), F32)] * TOP_K
                       + [pltpu.SemaphoreType.DMA((GATHER_AHEAD + 1,))],
        compiler_params=_cparams("arbitrary"),
        name="moe_combine_final",
    )(*([dest_flat] * (GATHER_AHEAD + 1)), x1_slab, prob, *ple_parts, y_slab, wpp, wpg, bpg, g2, b2)


def _routing_tables(top_e, pos, counts, n_blocks):
    c = MOE_ROWS
    e = jnp.arange(N_EXPERTS, dtype=jnp.int32)
    padded = (counts + c - 1) // c * c
    pad_end = jnp.sum(jnp.where(e[:, None] <= e[None, :], padded[:, None], 0), axis=0)
    pad_start = pad_end - padded
    start_of = jnp.sum(jnp.where(top_e[None] == e[:, None, None], pad_start[:, None, None], 0), axis=0)
    dest = (start_of + pos).T.reshape(-1)
    blk_first = jnp.arange(n_blocks, dtype=jnp.int32) * c
    blk_e = jnp.minimum(jnp.sum((pad_end[None, :] <= blk_first[:, None]).astype(jnp.int32), axis=1),
                        N_EXPERTS - 1)
    nact = pad_end[-1:] // c
    last_blk = jnp.where(padded > 0, pad_end // c - 1, -1)
    tail = nact + jnp.arange(n_blocks - top_e.size // c, dtype=jnp.int32)
    zero_blk = jnp.concatenate([last_blk, jnp.where(tail < n_blocks, tail, -1)])
    return dest.astype(jnp.int32), blk_e.astype(jnp.int32), nact.astype(jnp.int32), zero_blk.astype(jnp.int32)


def _layer(x_parts, ple_parts, lw, groups, out_rows, *, alpha, ln0):
    d = x_parts[0].shape[1]
    outs = _inproj(x_parts, ln0[0], ln0[1], lw['wg'], lw['wlr'], lw['wgk'], lw['bgk'], lw['wn'], apply_ln=ln0[2])
    if ln0[2]:
        x, outs = outs[0], outs[1:]
    else:
        (x,) = x_parts
    n = x.shape[0]
    qk, v, vt, gg, lg, qn, kn, vn = outs
    o_f, o_b = _gla(qk, v, vt, lg, groups)
    o_na = _na(qn, kn, vn, lw['na_bias'], groups)
    x1, top_e, prob, pos, cnt = _outproj(o_f, o_b, gg, o_na, x, lw['ng'], lw['wog'], lw['won'],
                                         lw['g1'], lw['b1'], lw['wrh'], lw['wrl'], lw['br'], alpha=alpha)
    n_blocks = (n * TOP_K + N_EXPERTS * (MOE_ROWS - 1) + MOE_ROWS - 1) // MOE_ROWS
    dest, blk_e, nact, zero_blk = _routing_tables(top_e[:TOP_K], pos[:TOP_K], cnt[:, 0], n_blocks)
    xs = _dispatch(zero_blk, dest, x1, n_blocks * MOE_ROWS, d // LANES)
    y = _experts(blk_e, nact, xs, lw['layer'], lw['wgu'], lw['bgu'], lw['wd'], lw['bd'])
    return _final(dest, x1, prob.T, ple_parts, lw['layer'], y, lw['wpp'], lw['wpg'], lw['bpg'], lw['g2'], lw['b2'],
                  out_rows, alpha=alpha)


def kernel(x_prompt, x_sample, p_prompt, p_sample, emb_ln_g, emb_ln_b, w_in, w_gk_f, b_gk_f, w_gk_b, b_gk_b, gla_norm_g, rpb, w_out, ln1_g, ln1_b, w_router, b_router, w_gu, b_gu, w_down, b_down, w_ple_proj, w_ple_gate, b_ple_gate, ln2_g, ln2_b):
    depth, d = w_in.shape[0], w_in.shape[1]
    alpha = float((2 * depth) ** 0.25)
    tb = TOKEN_BLOCK
    groups, first = [], 0
    for a in (x_prompt, x_sample):
        b, t, _ = a.shape
        assert t % tb == 0 and t % GRID_W == 0 and t // GRID_W >= NA_KH
        groups.append((first, t // tb))
        first += b * t // tb
    groups = tuple(groups)
    x_parts = [x_prompt.reshape(-1, d), x_sample.reshape(-1, d)]
    ple_parts = [p_prompt.reshape(depth, -1, p_prompt.shape[-1]), p_sample.reshape(depth, -1, p_sample.shape[-1])]
    n_all = sum(a.shape[0] for a in x_parts)
    row = lambda a: a.reshape(1, -1).astype(F32)
    c0, c1, c2 = 2 * GLA_QK + 2 * GLA_V, 2 * GLA_QK + 2 * GLA_V + 2 * GLA_RANK, w_in.shape[2]
    for i in range(depth):
        wgk = jnp.zeros((LANES, 2 * GLA_QK), F32)
        wgk = wgk.at[:GLA_RANK, :GLA_QK].set(w_gk_f[i]).at[GLA_RANK:2 * GLA_RANK, GLA_QK:].set(w_gk_b[i])
        wr = jnp.pad(w_router[i].astype(F32), ((0, 0), (0, LANES - N_EXPERTS)))
        lw = dict(
            wg=w_in[i, :, :c0].astype(BF16),
            wlr=jnp.pad(w_in[i, :, c0:c1], ((0, 0), (0, LANES - 2 * GLA_RANK))).astype(BF16),
            wgk=wgk.astype(BF16),
            bgk=jnp.concatenate([b_gk_f[i], b_gk_b[i]]).reshape(1, -1),
            wn=w_in[i, :, c1:c2].astype(BF16),
            na_bias=_na_bias_table(rpb[i]),
            ng=row(gla_norm_g[i]),
            wog=w_out[i, :GLA_V].astype(BF16), won=w_out[i, GLA_V:].astype(BF16),
            g1=row(ln1_g[i]), b1=row(ln1_b[i]),
            wrh=wr.astype(BF16), wrl=(wr - wr.astype(BF16).astype(F32)).astype(BF16),
            br=jnp.pad(b_router[i].astype(F32), (0, LANES - N_EXPERTS)).reshape(1, -1),
            layer=i, wgu=w_gu, bgu=b_gu[i][:, None, :], wd=w_down, bd=b_down[i][:, None, :],
            wpp=w_ple_proj[i].astype(BF16), wpg=w_ple_gate[i].astype(BF16), bpg=row(b_ple_gate[i]),
            g2=row(ln2_g[i]), b2=row(ln2_b[i]),
        )
        last = i == depth - 1
        out_rows = [a.shape[0] for a in x_parts] if last else [n_all]
        outs = _layer(x_parts if i == 0 else [x], ple_parts, lw, groups, out_rows,
                      alpha=alpha, ln0=(row(emb_ln_g), row(emb_ln_b), i == 0))
        if not last:
            (x,) = outs
    return (outs[0].reshape(x_prompt.shape), outs[1].reshape(x_sample.shape))
```

```python
import functools

import numpy as np
import jax
import jax.numpy as jnp
from jax import lax
from jax.experimental import pallas as pl
from jax.experimental.pallas import tpu as pltpu

GRID_W = 64
GLA_HEADS, GLA_DK, GLA_DV, GLA_RANK = 4, 64, 128, 16
GLA_NORMALIZER = 16.0
NA_HEADS, NA_DH, NA_KH, NA_KW = 8, 64, 8, 16
N_EXPERTS, TOP_K = 32, 4
SWIGLU_ALPHA, SWIGLU_LIMIT = 1.702, 7.0
LN_EPS, RMS_EPS = 1e-5, 1e-6

GLA_QK = GLA_HEADS * GLA_DK
GLA_V = GLA_HEADS * GLA_DV
NA_W = NA_HEADS * NA_DH

LANES = 128
SUBLANES = 8
V7X_VMEM_LIMIT_BYTES = 56 * 1024 * 1024

TOKEN_BLOCK = 1024
GLA_CHUNK = 128
GLA_SAFE_LOG_DECAY = 40.0
NA_ROWS_PER_BLOCK = TOKEN_BLOCK // GRID_W
NA_SCORE_LEAD = 4
MOE_ROWS = 1024
FINAL_BLOCK = 256
NEG_BIG = -1e30

F32 = jnp.float32
BF16 = jnp.bfloat16


def _cparams(*sem):
    return pltpu.CompilerParams(dimension_semantics=sem, vmem_limit_bytes=V7X_VMEM_LIMIT_BYTES)


def _dot(a, b, precision=None):
    return jnp.dot(a, b, preferred_element_type=F32, precision=precision)


def _dot_nt(a, b, precision=None):
    return lax.dot_general(a, b, (((1,), (1,)), ((), ())), preferred_element_type=F32, precision=precision)


def _dot_tn(a, b):
    return lax.dot_general(a, b, (((0,), (0,)), ((), ())), preferred_element_type=F32)


def _layer_norm(x, g, b):
    mu = jnp.mean(x, axis=-1, keepdims=True)
    xc = x - mu
    var = jnp.mean(xc * xc, axis=-1, keepdims=True)
    return xc * lax.rsqrt(var + LN_EPS) * g + b


def _slab_load(ref, n, width):
    per = width // LANES
    return jnp.concatenate([ref[pl.ds(s, n, stride=per), :] for s in range(per)], axis=-1)


def _slab_store(ref, x):
    n, width = x.shape
    per = width // LANES
    for s in range(per):
        ref[pl.ds(s, n, stride=per), :] = x[:, s * LANES:(s + 1) * LANES]


def _seq_local(blk, groups):
    loc = lax.rem(blk - groups[-1][0], groups[-1][1])
    bps = jnp.int32(groups[-1][1])
    for (first, per), nxt in zip(reversed(groups[:-1]), reversed(groups[1:])):
        inside = blk < nxt[0]
        loc = jnp.where(inside, lax.rem(blk - first, per), loc)
        bps = jnp.where(inside, per, bps)
    return loc, bps


def _part_specs(parts, block_rows, lead=None):
    specs, firsts, first = [], [], 0
    for a in parts:
        rows, width = a.shape[-2], a.shape[-1]
        nb = rows // block_rows
        if lead is None:
            spec = pl.BlockSpec((block_rows, width), lambda i, first=first, nb=nb: (jnp.clip(i - first, 0, nb - 1), 0))
        else:
            spec = pl.BlockSpec((None, block_rows, width),
                                lambda i, first=first, nb=nb: (lead, jnp.clip(i - first, 0, nb - 1), 0))
        specs.append(spec)
        firsts.append(first)
        first += nb
    return specs, tuple(firsts)


def _select_part(i, refs, firsts):
    x = refs[0][...]
    for ref, first in zip(refs[1:], firsts[1:]):
        x = jnp.where(i >= first, ref[...], x)
    return x


def _inproj_kernel(*refs, firsts, apply_ln):
    x_refs, refs = refs[:len(firsts)], refs[len(firsts):]
    g_ref, b_ref, wg_ref, wlr_ref, wgk_ref, bgk_ref, wn_ref = refs[:7]
    out_refs = refs[7:]
    if apply_ln:
        x0_ref, qk_ref, v_ref, vt_ref, gg_ref, lg_ref, qn_ref, kn_ref, vn_ref = out_refs
    else:
        qk_ref, v_ref, vt_ref, gg_ref, lg_ref, qn_ref, kn_ref, vn_ref = out_refs
    x = _select_part(pl.program_id(0), x_refs, firsts)
    if apply_ln:
        x = _layer_norm(x, g_ref[...], b_ref[...])
        x0_ref[...] = x
    _project_inputs(x, wg_ref, wlr_ref, wgk_ref, bgk_ref, wn_ref,
                    qk_ref, v_ref, vt_ref, gg_ref, lg_ref, qn_ref, kn_ref, vn_ref)


def _project_inputs(x, wg_ref, wlr_ref, wgk_ref, bgk_ref, wn_ref,
                    qk_ref, v_ref, vt_ref, gg_ref, lg_ref, qn_ref, kn_ref, vn_ref):
    xb = x.astype(BF16)
    zg = _dot(xb, wg_ref[...])
    qk_ref[:, :GLA_QK] = (zg[:, :GLA_QK] * GLA_DK ** -0.5).astype(BF16)
    qk_ref[:, GLA_QK:] = zg[:, GLA_QK:2 * GLA_QK].astype(BF16)
    v = zg[:, 2 * GLA_QK:2 * GLA_QK + GLA_V]
    v_ref[...] = v.astype(BF16)
    vt_ref[...] = v.T.astype(BF16)
    gg_ref[...] = zg[:, 2 * GLA_QK + GLA_V:]
    lr = _dot(xb, wlr_ref[...])
    pre = _dot(lr.astype(BF16), wgk_ref[...]) + bgk_ref[...]
    log_sig = jnp.minimum(pre, 0.0) - jnp.log1p(jnp.exp(-jnp.abs(pre)))
    lg_ref[...] = log_sig * (1.0 / GLA_NORMALIZER)
    zn = _dot(xb, wn_ref[...])
    qn_ref[...] = (zn[:, :NA_W] * NA_DH ** -0.5).astype(BF16)
    kn_ref[...] = zn[:, NA_W:2 * NA_W].astype(BF16)
    vn_ref[...] = zn[:, 2 * NA_W:].astype(BF16)


PROJECTION_OUTPUTS = ((GLA_QK * 2, BF16, False), (GLA_V, BF16, False), (GLA_V, BF16, True), (GLA_V, F32, False),
                      (2 * GLA_QK, F32, False), (NA_W, BF16, False), (NA_W, BF16, False), (NA_W, BF16, False))


def _inproj(x_parts, ln_g, ln_b, wg, wlr, wgk, bgk, wn, *, apply_ln):
    n = sum(a.shape[0] for a in x_parts)
    d = x_parts[0].shape[1]
    tm = TOKEN_BLOCK
    row = lambda w: pl.BlockSpec((tm, w), lambda i: (i, 0))
    full = lambda a: pl.BlockSpec(a.shape, lambda i: (0,) * a.ndim)
    x_specs, firsts = _part_specs(x_parts, tm)
    outs = list(PROJECTION_OUTPUTS)
    if apply_ln:
        outs = [(d, F32, False)] + outs
    return pl.pallas_call(
        functools.partial(_inproj_kernel, firsts=firsts, apply_ln=apply_ln),
        grid=(n // tm,),
        in_specs=x_specs + [full(ln_g), full(ln_b), full(wg), full(wlr), full(wgk), full(bgk), full(wn)],
        out_specs=[pl.BlockSpec((w, tm), lambda i: (0, i)) if tr else row(w) for w, _, tr in outs],
        out_shape=[jax.ShapeDtypeStruct((w, n) if tr else (n, w), t) for w, t, tr in outs],
        compiler_params=_cparams("arbitrary"),
        name="inproj_ln" if apply_ln else "inproj",
    )(*x_parts, ln_g, ln_b, wg, wlr, wgk, bgk, wn)


GLA_HEADS_PER_TILE = LANES // GLA_DK
assert GLA_DV == LANES and GLA_HEADS % GLA_HEADS_PER_TILE == 0


def _head_lane_masks(rows):
    lane = lax.broadcasted_iota(jnp.int32, (rows, LANES), 1)
    return [(lane >= hh * GLA_DK) & (lane < (hh + 1) * GLA_DK) for hh in range(GLA_HEADS_PER_TILE)]


def _gla_chunk_fast(qk_ref, v_ref, vt_ref, lg_ref, o_ref, s_ref, d, c, fwd):
    C = GLA_CHUNK
    rows = pl.ds(c * C, C)
    r_i = lax.broadcasted_iota(jnp.int32, (C, C), 0)
    c_i = lax.broadcasted_iota(jnp.int32, (C, C), 1)
    tri = ((r_i >= c_i) if fwd else (r_i <= c_i)).astype(BF16)
    lg = lg_ref[rows, :]
    lg1 = lg.astype(BF16)
    rem = lg - lg1.astype(F32)
    lg2 = rem.astype(BF16)
    lg3 = (rem - lg2.astype(F32)).astype(BF16)
    cum = _dot(tri, lg1) + (_dot(tri, lg2) + _dot(tri, lg3))
    yield
    tot = cum[C - 1:C, :] if fwd else cum[0:1, :]
    q = qk_ref[rows, :GLA_QK].astype(F32)
    k = qk_ref[rows, GLA_QK:].astype(F32)
    qt = (q * jnp.exp(cum)).astype(BF16)
    kt = (k * jnp.exp(-cum)).astype(BF16)
    kd = (k * jnp.exp(tot - cum)).astype(BF16)
    dec = jnp.exp(tot)
    mask = (r_i >= c_i) if fwd else (r_i < c_i)
    in_head = _head_lane_masks(C)
    zero = jnp.zeros((C, LANES), BF16)
    tiles = [slice(p * LANES, (p + 1) * LANES) for p in range(GLA_HEADS // GLA_HEADS_PER_TILE)]
    yield
    states, aos = [], []
    for p, tile in enumerate(tiles):
        s = s_ref[d, p]
        q_heads = jnp.concatenate([jnp.where(m, qt[:, tile], zero) for m in in_head], axis=0)
        aos.append(_dot_nt(q_heads, jnp.concatenate([kt[:, tile], s.astype(BF16)], axis=0)))
        states.append(s)
        yield
    for p, tile in enumerate(tiles):
        vt = [vt_ref[(p * GLA_HEADS_PER_TILE + hh) * GLA_DV:(p * GLA_HEADS_PER_TILE + hh + 1) * GLA_DV,
                     c * C:(c + 1) * C] for hh in range(GLA_HEADS_PER_TILE)]
        kd_heads = jnp.concatenate([jnp.where(m, kd[:, tile], zero) for m in in_head], axis=0)
        s_ref[d, p] = dec[:, tile] * states[p] + _dot(jnp.concatenate(vt, axis=1), kd_heads)
        yield
    for p in range(len(tiles)):
        for hh in range(GLA_HEADS_PER_TILE):
            h = p * GLA_HEADS_PER_TILE + hh
            vs = slice(h * GLA_DV, (h + 1) * GLA_DV)
            a = jnp.where(mask, aos[p][hh * C:(hh + 1) * C, :C], 0.0).astype(BF16)
            o_ref[rows, vs] = _dot(a, v_ref[rows, vs]) + aos[p][hh * C:(hh + 1) * C, C:]
            yield


GLA_CHUNK_STAGGER = 3
assert GLA_CHUNK_STAGGER >= GLA_HEADS // GLA_HEADS_PER_TILE


def _interleave(staggered):
    pending = sorted(staggered, key=lambda item: item[0])
    live, tick = [], 0
    while pending or live:
        while pending and pending[0][0] <= tick:
            live.append(pending.pop(0)[1])
        for g in list(live):
            try:
                next(g)
            except StopIteration:
                live.remove(g)
        tick += 1


def _gla_block_slow(qk_ref, v_ref, lg_ref, o_ref, s_ref, d, fwd, tb):
    G = 16

    def group(n, carry):
        rows = pl.ds(pl.multiple_of((n if fwd else tb // G - 1 - n) * G, G), G)
        gate = jnp.exp(lg_ref[rows, :])
        q = qk_ref[rows, :GLA_QK].astype(F32)
        k = qk_ref[rows, GLA_QK:]
        qs = (q if fwd else q * gate).astype(BF16)
        row = lax.broadcasted_iota(jnp.int32, (G, LANES), 0)
        in_head = _head_lane_masks(G)
        zero = jnp.zeros((G, LANES), BF16)
        for p in range(GLA_HEADS // GLA_HEADS_PER_TILE):
            tile = slice(p * LANES, (p + 1) * LANES)
            heads = [p * GLA_HEADS_PER_TILE + hh for hh in range(GLA_HEADS_PER_TILE)]
            vs = [slice(h * GLA_DV, (h + 1) * GLA_DV) for h in heads]
            v = [v_ref[rows, sl] for sl in vs]
            k_heads = [jnp.where(m, k[:, tile], zero) for m in in_head]
            s = s_ref[d, p]
            o = [jnp.zeros((G, GLA_DV), F32) for _ in heads]
            for r in (range(G) if fwd else reversed(range(G))):
                upd = None
                for hh in range(len(heads)):
                    term = _dot_tn(jnp.where(row == r, v[hh], zero), k_heads[hh])
                    upd = term if upd is None else upd + term
                s_new = gate[r:r + 1, tile] * s + upd
                s_read = (s_new if fwd else s).astype(BF16)
                for hh in range(len(heads)):
                    o[hh] = o[hh] + _dot_nt(jnp.where((row == r) & in_head[hh], qs[:, tile], zero), s_read)
                s = s_new
            for hh in range(len(heads)):
                o_ref[rows, vs[hh]] = o[hh]
            s_ref[d, p] = s
        return carry

    lax.fori_loop(0, tb // G, group, 0)


def _gla_kernel(qkf_ref, vf_ref, vtf_ref, lgf_ref, qkb_ref, vb_ref, vtb_ref, lgb_ref, of_ref, ob_ref, s_ref,
                *, groups, tb):
    i = pl.program_id(0)
    nblk = pl.num_programs(0)
    loc_f, _ = _seq_local(i, groups)
    loc_b, bps_b = _seq_local(nblk - 1 - i, groups)

    @pl.when(loc_f == 0)
    def _():
        s_ref[0] = jnp.zeros(s_ref.shape[1:], F32)

    @pl.when(loc_b == bps_b - 1)
    def _():
        s_ref[1] = jnp.zeros(s_ref.shape[1:], F32)

    nc = tb // GLA_CHUNK
    worst = jnp.float32(0.0)
    for ref in (lgf_ref, lgb_ref):
        for c in range(nc):
            tot = jnp.sum(ref[pl.ds(c * GLA_CHUNK, GLA_CHUNK), :], axis=0, keepdims=True)
            worst = jnp.minimum(worst, jnp.min(tot))
    safe = worst > -GLA_SAFE_LOG_DECAY

    @pl.when(safe)
    def _():
        work = []
        for c in range(nc):
            work.append((c * GLA_CHUNK_STAGGER,
                         _gla_chunk_fast(qkf_ref, vf_ref, vtf_ref, lgf_ref, of_ref, s_ref, 0, c, True)))
            work.append((c * GLA_CHUNK_STAGGER,
                         _gla_chunk_fast(qkb_ref, vb_ref, vtb_ref, lgb_ref, ob_ref, s_ref, 1, nc - 1 - c, False)))
        _interleave(work)

    @pl.when(jnp.logical_not(safe))
    def _():
        _gla_block_slow(qkf_ref, vf_ref, lgf_ref, of_ref, s_ref, 0, True, tb)
        _gla_block_slow(qkb_ref, vb_ref, lgb_ref, ob_ref, s_ref, 1, False, tb)


def _gla(qk, v, vt, lg, groups):
    n = qk.shape[0]
    tb = TOKEN_BLOCK
    nblk = n // tb
    fwd = lambda w, j: pl.BlockSpec((tb, w), lambda i: (i, j))
    bwd = lambda w, j: pl.BlockSpec((tb, w), lambda i: (nblk - 1 - i, j))
    return pl.pallas_call(
        functools.partial(_gla_kernel, groups=groups, tb=tb),
        grid=(nblk,),
        in_specs=[fwd(2 * GLA_QK, 0), fwd(GLA_V, 0), pl.BlockSpec((GLA_V, tb), lambda i: (0, i)), fwd(GLA_QK, 0),
                  bwd(2 * GLA_QK, 0), bwd(GLA_V, 0), pl.BlockSpec((GLA_V, tb), lambda i: (0, nblk - 1 - i)),
                  bwd(GLA_QK, 1)],
        out_specs=[fwd(GLA_V, 0), bwd(GLA_V, 0)],
        out_shape=[jax.ShapeDtypeStruct((n, GLA_V), F32)] * 2,
        scratch_shapes=[pltpu.VMEM((2, GLA_HEADS // GLA_HEADS_PER_TILE, GLA_DV, LANES), F32)],
        compiler_params=_cparams("arbitrary"),
        name="gla",
    )(qk, v, vt, lg, qk, v, vt, lg)


def _na_bias_table(rpb):
    w = GRID_W
    col = np.arange(w)
    start = np.clip(col - NA_KW // 2, 0, w - NA_KW)
    kc = np.arange(w)
    in_win = (kc[None, :] >= start[:, None]) & (kc[None, :] < start[:, None] + NA_KW)
    t = jnp.stack([rpb[:, NA_KH - 1 - s:2 * NA_KH - 1 - s] for s in range(NA_KH)], axis=1)
    t = jnp.pad(t, ((0, 0), (0, 0), (0, 0), (w, w)))
    t = jnp.stack([t[..., w + NA_KW - 1 - c:2 * w + NA_KW - 1 - c] for c in range(w)], axis=3)
    t = jnp.where(jnp.asarray(in_win)[None, None, None], t, NEG_BIG)
    t = t.transpose(1, 0, 3, 2, 4)
    heads = LANES // NA_DH
    return t.reshape(NA_KH, NA_HEADS // heads, heads * w, NA_KH * w).astype(F32)


def _na_window_start(blk, tb, n_tokens):
    return jnp.clip((blk - 1) * tb, 0, n_tokens - 3 * tb)


def _na_kernel(q_ref, kwin_ref, vwin_ref, bias_ref, o_ref, *, groups, n_tokens):
    tb = TOKEN_BLOCK
    w = GRID_W
    rpb_rows = NA_ROWS_PER_BLOCK
    blk = pl.program_id(1)
    loc, bps = _seq_local(blk, groups)
    rows_in_seq = bps * rpb_rows
    seq_first = (blk - loc) * tb - _na_window_start(blk, tb, n_tokens)
    lane = lax.broadcasted_iota(jnp.int32, (w, LANES), 1)
    heads = LANES // NA_DH
    in_head = [(lane >= hh * NA_DH) & (lane < (hh + 1) * NA_DH) for hh in range(heads)]
    win = NA_KH * w

    def probs(j):
        r = loc * rpb_rows + j
        rs = jnp.clip(r - NA_KH // 2, 0, rows_in_seq - NA_KH)
        off = pl.multiple_of(seq_first + rs * w, w)
        q = q_ref[pl.ds(j * w, w), :]
        q_heads = jnp.concatenate([jnp.where(m, q, jnp.zeros_like(q)) for m in in_head], axis=0)
        s = _dot_nt(q_heads, kwin_ref[pl.ds(off, win), :]) + bias_ref[r - rs, 0]
        p = jnp.exp(s - jnp.max(s, axis=-1, keepdims=True))
        return j, off, p.astype(BF16), jnp.sum(p, axis=-1, keepdims=True)

    def output(j, off, p, l):
        o = _dot(p, vwin_ref[pl.ds(off, win), :]) * (1.0 / l)
        out = o[:w]
        for hh in range(1, heads):
            out = jnp.where(in_head[hh], o[hh * w:(hh + 1) * w], out)
        o_ref[pl.ds(j * w, w), :] = out.astype(o_ref.dtype)

    pending = []
    for j in range(rpb_rows):
        pending.append(probs(j))
        if len(pending) > NA_SCORE_LEAD:
            output(*pending.pop(0))
    for item in pending:
        output(*item)


def _na(q, k, v, bias, groups):
    n = q.shape[0]
    tb = TOKEN_BLOCK
    nblk = n // tb
    nhp = NA_W // LANES
    cur = pl.BlockSpec((tb, LANES), lambda hp, i: (i, hp))
    window = pl.BlockSpec((pl.Element(3 * tb), pl.Element(LANES)),
                          lambda hp, i: (pl.multiple_of(_na_window_start(i, tb, n), tb),
                                         pl.multiple_of(hp * LANES, LANES)))
    return pl.pallas_call(
        functools.partial(_na_kernel, groups=groups, n_tokens=n),
        grid=(nhp, nblk),
        in_specs=[cur, window, window,
                  pl.BlockSpec((NA_KH, 1) + bias.shape[2:], lambda hp, i: (0, hp, 0, 0))],
        out_specs=cur,
        out_shape=jax.ShapeDtypeStruct((n, NA_W), BF16),
        compiler_params=_cparams("parallel", "parallel"),
        name="natten",
    )(q, k, v, bias)


OUTPROJ_SUB_BLOCKS = 4
OUTPROJ_STAGGER = 0


def _outproj_kernel(of_ref, ob_ref, gg_ref, ona_ref, x_ref, ng_ref, wog_ref, won_ref, g1_ref, b1_ref,
                    wrh_ref, wrl_ref, br_ref, x1_ref, e_ref, p_ref, pos_ref, cnt_ref, run_ref, ut_ref,
                    *, alpha):
    i = pl.program_id(0)
    tm, d = x_ref.shape
    th = tm // OUTPROJ_SUB_BLOCKS
    per = d // LANES

    @pl.when(i == 0)
    def _():
        run_ref[...] = jnp.zeros(run_ref.shape, F32)
        r_i = lax.broadcasted_iota(jnp.int32, (th, th), 0)
        c_i = lax.broadcasted_iota(jnp.int32, (th, th), 1)
        ut_ref[...] = (r_i < c_i).astype(BF16)

    runs = [run_ref[...]]

    def sub_block(j):
        rows = pl.ds(j * th, th)
        o = of_ref[rows, :] + ob_ref[rows, :]
        parts = []
        for h in range(GLA_HEADS):
            oh = o[:, h * GLA_DV:(h + 1) * GLA_DV]
            parts.append(oh * lax.rsqrt(jnp.mean(oh * oh, axis=-1, keepdims=True) + RMS_EPS))
        g = gg_ref[rows, :]
        o = jnp.concatenate(parts, axis=-1) * ng_ref[...] * (g * jax.nn.sigmoid(g))
        yield
        mix = _dot(o.astype(BF16), wog_ref[...]) + _dot(ona_ref[rows, :], won_ref[...])
        yield
        x1 = _layer_norm(alpha * x_ref[rows, :] + mix, g1_ref[...], b1_ref[...])
        _slab_store(x1_ref.at[pl.ds(j * th * per, th * per)], x1)
        x_hi = x1.astype(BF16)
        x_lo = (x1 - x_hi.astype(F32)).astype(BF16)
        yield
        logits = (_dot(x_hi, wrh_ref[...]) + (_dot(x_hi, wrl_ref[...]) + _dot(x_lo, wrh_ref[...])) + br_ref[...])
        yield
        lg = logits.T[:N_EXPERTS]
        eid = lax.broadcasted_iota(jnp.int32, (N_EXPERTS, th), 0)
        out_row = lax.broadcasted_iota(jnp.int32, (SUBLANES, th), 0)
        chosen = jnp.zeros((N_EXPERTS, th), F32)
        e_out = jnp.zeros((SUBLANES, th), jnp.int32)
        vals, sels = [], []
        for k in range(TOP_K):
            m = jnp.max(lg, axis=0, keepdims=True)
            idx = jnp.min(jnp.where(lg == m, eid, N_EXPERTS), axis=0, keepdims=True)
            sel = eid == idx
            sels.append(sel)
            vals.append(m)
            e_out = jnp.where(out_row == k, idx, e_out)
            chosen = jnp.where(sel, 1.0, chosen)
            lg = jnp.where(sel, -jnp.inf, lg)
        ex = [jnp.exp(v - vals[0]) for v in vals]
        inv = 1.0 / functools.reduce(lambda a, b: a + b, ex)
        p_out = jnp.zeros((SUBLANES, th), F32)
        for k in range(TOP_K):
            p_out = jnp.where(out_row == k, ex[k] * inv, p_out)
        p_ref[:, rows] = p_out
        e_ref[:, rows] = e_out
        counts = jnp.sum(chosen, axis=1, keepdims=True)
        yield
        while len(runs) <= j:
            yield
        before = _dot(chosen.astype(BF16), ut_ref[...]) + runs[j][:, :1]
        runs.append(runs[j] + counts)
        pos = jnp.zeros((SUBLANES, th), jnp.int32)
        for k in range(TOP_K):
            pk = jnp.sum(jnp.where(sels[k], before, 0.0), axis=0, keepdims=True)
            pos = jnp.where(out_row == k, pk.astype(jnp.int32), pos)
        pos_ref[:, rows] = pos

    _interleave([(j * OUTPROJ_STAGGER, sub_block(j)) for j in range(OUTPROJ_SUB_BLOCKS)])
    run_ref[...] = runs[-1]
    cnt_ref[...] = runs[-1].astype(jnp.int32)


def _outproj(o_f, o_b, gg, o_na, x, ng, wog, won, g1, b1, wrh, wrl, br, *, alpha):
    n, d = x.shape
    tm = TOKEN_BLOCK
    row = lambda w: pl.BlockSpec((tm, w), lambda i: (i, 0))
    col = pl.BlockSpec((SUBLANES, tm), lambda i: (0, i))
    full = lambda a: pl.BlockSpec(a.shape, lambda i: (0,) * a.ndim)
    return pl.pallas_call(
        functools.partial(_outproj_kernel, alpha=alpha),
        grid=(n // tm,),
        in_specs=[row(GLA_V), row(GLA_V), row(GLA_V), row(NA_W), row(d), full(ng), full(wog), full(won),
                  full(g1), full(b1), full(wrh), full(wrl), full(br)],
        out_specs=[pl.BlockSpec((tm * d // LANES, LANES), lambda i: (i, 0)), col, col, col,
                   pl.BlockSpec((N_EXPERTS, LANES), lambda i: (0, 0))],
        out_shape=[jax.ShapeDtypeStruct((n * d // LANES, LANES), F32),
                   jax.ShapeDtypeStruct((SUBLANES, n),jnp.int32), jax.ShapeDtypeStruct((SUBLANES, n),F32),
                   jax.ShapeDtypeStruct((SUBLANES, n),jnp.int32), jax.ShapeDtypeStruct((N_EXPERTS, LANES), jnp.int32)],
        scratch_shapes=[pltpu.VMEM((N_EXPERTS, LANES), F32),
                        pltpu.VMEM((tm // OUTPROJ_SUB_BLOCKS, tm // OUTPROJ_SUB_BLOCKS), BF16)],
        compiler_params=_cparams("arbitrary"),
        name="outproj_router",
    )(o_f, o_b, gg, o_na, x, ng, wog, won, g1, b1, wrh, wrl, br)


def _slab_rows(ref, row, per):
    return ref.at[pl.ds(pl.multiple_of(row * per, per), per)]


DMA_ISSUE_UNROLL = 8


def _dispatch_kernel(zero_blk_ref, dest_ref, x_ref, xs_ref, zbuf, sem, *, per):
    tm = x_ref.shape[0] // per
    c = MOE_ROWS

    @pl.when(pl.program_id(0) == 0)
    def _():
        zbuf[...] = jnp.zeros(zbuf.shape, zbuf.dtype)

        def fill(j):
            return pltpu.make_async_copy(
                zbuf, xs_ref.at[pl.ds(pl.multiple_of(zero_blk_ref[j] * (c * per), c * per), c * per)], sem)

        def start(j, carry):
            @pl.when(zero_blk_ref[j] >= 0)
            def _():
                fill(j).start()
            return carry

        def wait(j, carry):
            @pl.when(zero_blk_ref[j] >= 0)
            def _():
                fill(j).wait()
            return carry

        lax.fori_loop(0, zero_blk_ref.shape[0], start, 0)
        lax.fori_loop(0, zero_blk_ref.shape[0], wait, 0)

    def group(g, carry):
        for u in range(DMA_ISSUE_UNROLL):
            t = g * DMA_ISSUE_UNROLL + u
            for k in range(TOP_K):
                pltpu.make_async_copy(_slab_rows(x_ref, t, per), _slab_rows(xs_ref, dest_ref[t * TOP_K + k], per),
                                      sem).start(priority=k % 2)
        return carry

    lax.fori_loop(0, tm // DMA_ISSUE_UNROLL, group, 0)
    for k in range(TOP_K):
        pltpu.make_async_copy(x_ref, xs_ref.at[pl.ds(0, tm * per)], sem).wait()


def _dispatch(zero_blk, dest_flat, x1_slab, n_slots, per):
    n = x1_slab.shape[0] // per
    tm = TOKEN_BLOCK
    grid_spec = pltpu.PrefetchScalarGridSpec(
        num_scalar_prefetch=1,
        grid=(n // tm,),
        in_specs=[pl.BlockSpec((tm * TOP_K,), lambda i, zb: (i,), memory_space=pltpu.SMEM),
                  pl.BlockSpec((tm * per, LANES), lambda i, zb: (i, 0))],
        out_specs=pl.BlockSpec(memory_space=pl.ANY),
        scratch_shapes=[pltpu.VMEM((MOE_ROWS * per, LANES), x1_slab.dtype), pltpu.SemaphoreType.DMA(())],
    )
    return pl.pallas_call(
        functools.partial(_dispatch_kernel, per=per),
        grid_spec=grid_spec,
        out_shape=jax.ShapeDtypeStruct((n_slots * per, LANES), x1_slab.dtype),
        compiler_params=_cparams("arbitrary"),
        name="moe_dispatch",
    )(zero_blk, dest_flat, x1_slab)


def _experts_kernel(be_ref, nact_ref, xs_ref, wgu_ref, bgu_ref, wd_ref, bd_ref, y_ref, wgu_bf, wd_bf):
    i = pl.program_id(0)
    d, de = wd_ref.shape[2], wd_ref.shape[1]
    c = MOE_ROWS

    @pl.when((i == 0) | (be_ref[i] != be_ref[jnp.maximum(i - 1, 0)]))
    def _():
        wgu_bf[...] = wgu_ref[0].astype(BF16)
        wd_bf[...] = wd_ref[0].astype(BF16)

    @pl.when(i < nact_ref[0])
    def _():
        x = _slab_load(xs_ref, c, d).astype(BF16)
        h = _dot(x, wgu_bf[...]) + bgu_ref[0]
        gate = jnp.minimum(h[:, :de], SWIGLU_LIMIT)
        up = jnp.clip(h[:, de:], -SWIGLU_LIMIT, SWIGLU_LIMIT)
        glu = gate * jax.nn.sigmoid(gate * SWIGLU_ALPHA)
        act = ((up + 1.0) * glu).astype(BF16)
        _slab_store(y_ref, _dot(act, wd_bf[...]) + bd_ref[0])

    @pl.when(i >= nact_ref[0])
    def _():
        y_ref[...] = jnp.zeros(y_ref.shape, y_ref.dtype)


def _experts(blk_e, nact, xs, layer, wgu, bgu, wd, bd):
    d, de = wd.shape[3], wd.shape[2]
    per = d // LANES
    c = MOE_ROWS
    slab = pl.BlockSpec((c * per, LANES), lambda i, be, na: (i, 0))
    grid_spec = pltpu.PrefetchScalarGridSpec(
        num_scalar_prefetch=2,
        grid=(xs.shape[0] // (c * per),),
        in_specs=[slab,
                  pl.BlockSpec((None, 1, d, 2 * de), lambda i, be, na: (layer, be[i], 0, 0)),
                  pl.BlockSpec((1, 1, 2 * de), lambda i, be, na: (be[i], 0, 0)),
                  pl.BlockSpec((None, 1, de, d), lambda i, be, na: (layer, be[i], 0, 0)),
                  pl.BlockSpec((1, 1, d), lambda i, be, na: (be[i], 0, 0))],
        out_specs=slab,
        scratch_shapes=[pltpu.VMEM((d, 2 * de), BF16), pltpu.VMEM((de, d), BF16)],
    )
    return pl.pallas_call(
        _experts_kernel,
        grid_spec=grid_spec,
        out_shape=jax.ShapeDtypeStruct(xs.shape, F32),
        compiler_params=_cparams("arbitrary"),
        name="moe_experts",
    )(blk_e, nact, xs, wgu, bgu, wd, bd)


GATHER_AHEAD = 2


def _final_kernel(*refs, ple_firsts, out_firsts, nblk, project_next, alpha):
    dest_refs, refs = refs[:GATHER_AHEAD + 1], refs[GATHER_AHEAD + 1:]
    x1_ref, prob_ref = refs[:2]
    refs = refs[2:]
    ple_refs, refs = refs[:len(ple_firsts)], refs[len(ple_firsts):]
    y_ref, wpp_ref, wpg_ref, bpg_ref, g2_ref, b2_ref = refs[:6]
    refs = refs[6:]
    proj_w_refs, refs = (refs[:5], refs[5:]) if project_next else ((), refs)
    out_refs, refs = refs[:len(out_firsts)], refs[len(out_firsts):]
    n_proj_out = len(PROJECTION_OUTPUTS) if project_next else 0
    proj_out_refs, scratch = refs[:n_proj_out], refs[n_proj_out:]
    ybufs, sems = scratch[:TOP_K], scratch[TOP_K]
    tm, d = out_refs[0].shape
    per = d // LANES
    i = pl.program_id(0)
    nbuf = GATHER_AHEAD + 1
    slot = lax.rem(i, nbuf)

    def gather(idx_ref, into):
        def group(g, carry):
            for u in range(DMA_ISSUE_UNROLL):
                t = g * DMA_ISSUE_UNROLL + u
                for k in range(TOP_K):
                    pltpu.make_async_copy(_slab_rows(y_ref, idx_ref[t * TOP_K + k], per),
                                          _slab_rows(ybufs[k], into * tm + t, per),
                                          sems.at[into]).start(priority=k % 2)
            return carry
        lax.fori_loop(0, tm // DMA_ISSUE_UNROLL, group, 0)

    @pl.when(i == 0)
    def _():
        for a in range(min(GATHER_AHEAD, nblk)):
            gather(dest_refs[a], a)

    @pl.when(i + GATHER_AHEAD < nblk)
    def _():
        gather(dest_refs[GATHER_AHEAD], lax.rem(i + GATHER_AHEAD, nbuf))

    half = pl.ds(pl.multiple_of(slot * (tm * per), tm * per), tm * per)
    for k in range(TOP_K):
        pltpu.make_async_copy(y_ref.at[pl.ds(0, tm * per)], ybufs[k].at[half], sems.at[slot]).wait()
    prob = prob_ref[...]
    moe = None
    for k in range(TOP_K):
        yk = _slab_load(ybufs[k].at[half], tm, d)
        moe = prob[:, k:k + 1] * yk if moe is None else moe + prob[:, k:k + 1] * yk
    r = alpha * _slab_load(x1_ref, tm, d) + moe
    gate = jax.nn.sigmoid(_dot(r.astype(BF16), wpg_ref[...]) + bpg_ref[...])
    ple = _select_part(i, ple_refs, ple_firsts)
    u = _dot(ple.astype(BF16), wpp_ref[...]) * gate
    res = _layer_norm(r + u, g2_ref[...], b2_ref[...])
    if project_next:
        _project_inputs(res, *proj_w_refs, *proj_out_refs)
    if len(out_refs) == 1:
        out_refs[0][...] = res
    else:
        bounds = list(out_firsts[1:]) + [pl.num_programs(0)]
        for ref, first, end in zip(out_refs, out_firsts, bounds):
            @pl.when((i >= first) & (i < end))
            def _(ref=ref):
                ref[...] = res


def _final(dest_flat, x1_slab, prob, ple_parts, layer, y_slab, wpp, wpg, bpg, g2, b2, out_rows, next_proj, *, alpha):
    n = prob.shape[0]
    d = wpg.shape[0]
    per = d // LANES
    tm = FINAL_BLOCK
    nblk = n // tm
    row = lambda w: pl.BlockSpec((tm, w), lambda i: (i, 0))
    full = lambda a: pl.BlockSpec(a.shape, lambda i: (0,) * a.ndim)
    ple_specs, ple_firsts = _part_specs(ple_parts, tm, lead=layer)
    out_shapes = [jax.ShapeDtypeStruct((r, d), F32) for r in out_rows]
    out_specs, out_firsts = _part_specs(out_shapes, tm)
    proj_w = list(next_proj) if next_proj is not None else []
    if proj_w:
        assert len(out_rows) == 1
        out_specs = out_specs + [pl.BlockSpec((w, tm), lambda i: (0, i)) if tr else row(w)
                                 for w, _, tr in PROJECTION_OUTPUTS]
        out_shapes = out_shapes + [jax.ShapeDtypeStruct((w, n) if tr else (n, w), t) for w, t, tr in PROJECTION_OUTPUTS]
    return pl.pallas_call(
        functools.partial(_final_kernel, ple_firsts=ple_firsts, out_firsts=out_firsts, nblk=nblk,
                          project_next=bool(proj_w), alpha=alpha),
        grid=(nblk,),
        in_specs=[pl.BlockSpec((tm * TOP_K,), lambda i, a=a: (jnp.minimum(i + a, nblk - 1),), memory_space=pltpu.SMEM)
                  for a in range(GATHER_AHEAD + 1)]
                 + [pl.BlockSpec((tm * per, LANES), lambda i: (i, 0)), row(prob.shape[1])]
                 + ple_specs
                 + [pl.BlockSpec(memory_space=pl.ANY), full(wpp), full(wpg), full(bpg), full(g2), full(b2)]
                 + [full(a) for a in proj_w],
        out_specs=out_specs,
        out_shape=out_shapes,
        scratch_shapes=[pltpu.VMEM(((GATHER_AHEAD + 1) * tm * per, LANES), F32)] * TOP_K
                       + [pltpu.SemaphoreType.DMA((GATHER_AHEAD + 1,))],
        compiler_params=_cparams("arbitrary"),
        name="moe_combine_final",
    )(*([dest_flat] * (GATHER_AHEAD + 1)), x1_slab, prob, *ple_parts, y_slab, wpp, wpg, bpg, g2, b2, *proj_w)


def _routing_tables(top_e, pos, counts, n_blocks):
    c = MOE_ROWS
    e = jnp.arange(N_EXPERTS, dtype=jnp.int32)
    padded = (counts + c - 1) // c * c
    pad_end = jnp.sum(jnp.where(e[:, None] <= e[None, :], padded[:, None], 0), axis=0)
    pad_start = pad_end - padded
    start_of = jnp.sum(jnp.where(top_e[None] == e[:, None, None], pad_start[:, None, None], 0), axis=0)
    dest = (start_of + pos).T.reshape(-1)
    blk_first = jnp.arange(n_blocks, dtype=jnp.int32) * c
    blk_e = jnp.minimum(jnp.sum((pad_end[None, :] <= blk_first[:, None]).astype(jnp.int32), axis=1),
                        N_EXPERTS - 1)
    nact = pad_end[-1:] // c
    last_blk = jnp.where(padded > 0, pad_end // c - 1, -1)
    tail = nact + jnp.arange(n_blocks - top_e.size // c, dtype=jnp.int32)
    zero_blk = jnp.concatenate([last_blk, jnp.where(tail < n_blocks, tail, -1)])
    return dest.astype(jnp.int32), blk_e.astype(jnp.int32), nact.astype(jnp.int32), zero_blk.astype(jnp.int32)


def _layer(x_parts, projected, ple_parts, lw, next_lw, groups, out_rows, *, alpha, ln0):
    d = x_parts[0].shape[1]
    if projected is None:
        outs = _inproj(x_parts, ln0[0], ln0[1], lw['wg'], lw['wlr'], lw['wgk'], lw['bgk'], lw['wn'], apply_ln=ln0[2])
        if ln0[2]:
            x, outs = outs[0], outs[1:]
        else:
            (x,) = x_parts
    else:
        (x,), outs = x_parts, projected
    n = x.shape[0]
    qk, v, vt, gg, lg, qn, kn, vn = outs
    o_f, o_b = _gla(qk, v, vt, lg, groups)
    o_na = _na(qn, kn, vn, lw['na_bias'], groups)
    x1, top_e, prob, pos, cnt = _outproj(o_f, o_b, gg, o_na, x, lw['ng'], lw['wog'], lw['won'],
                                         lw['g1'], lw['b1'], lw['wrh'], lw['wrl'], lw['br'], alpha=alpha)
    n_blocks = (n * TOP_K + N_EXPERTS * (MOE_ROWS - 1) + MOE_ROWS - 1) // MOE_ROWS
    dest, blk_e, nact, zero_blk = _routing_tables(top_e[:TOP_K], pos[:TOP_K], cnt[:, 0], n_blocks)
    xs = _dispatch(zero_blk, dest, x1, n_blocks * MOE_ROWS, d // LANES)
    y = _experts(blk_e, nact, xs, lw['layer'], lw['wgu'], lw['bgu'], lw['wd'], lw['bd'])
    next_proj = None if next_lw is None else [next_lw[k] for k in ('wg', 'wlr', 'wgk', 'bgk', 'wn')]
    return _final(dest, x1, prob.T, ple_parts, lw['layer'], y, lw['wpp'], lw['wpg'], lw['bpg'], lw['g2'], lw['b2'],
                  out_rows, next_proj, alpha=alpha)


def kernel(x_prompt, x_sample, p_prompt, p_sample, emb_ln_g, emb_ln_b, w_in, w_gk_f, b_gk_f, w_gk_b, b_gk_b, gla_norm_g, rpb, w_out, ln1_g, ln1_b, w_router, b_router, w_gu, b_gu, w_down, b_down, w_ple_proj, w_ple_gate, b_ple_gate, ln2_g, ln2_b):
    depth, d = w_in.shape[0], w_in.shape[1]
    alpha = float((2 * depth) ** 0.25)
    tb = TOKEN_BLOCK
    groups, first = [], 0
    for a in (x_prompt, x_sample):
        b, t, _ = a.shape
        assert t % tb == 0 and t % GRID_W == 0 and t // GRID_W >= NA_KH
        groups.append((first, t // tb))
        first += b * t // tb
    groups = tuple(groups)
    x_parts = [x_prompt.reshape(-1, d), x_sample.reshape(-1, d)]
    ple_parts = [p_prompt.reshape(depth, -1, p_prompt.shape[-1]), p_sample.reshape(depth, -1, p_sample.shape[-1])]
    n_all = sum(a.shape[0] for a in x_parts)
    row = lambda a: a.reshape(1, -1).astype(F32)
    c0, c1, c2 = 2 * GLA_QK + 2 * GLA_V, 2 * GLA_QK + 2 * GLA_V + 2 * GLA_RANK, w_in.shape[2]
    layers = []
    for i in range(depth):
        wgk = jnp.zeros((LANES, 2 * GLA_QK), F32)
        wgk = wgk.at[:GLA_RANK, :GLA_QK].set(w_gk_f[i]).at[GLA_RANK:2 * GLA_RANK, GLA_QK:].set(w_gk_b[i])
        wr = jnp.pad(w_router[i].astype(F32), ((0, 0), (0, LANES - N_EXPERTS)))
        lw = dict(
            wg=w_in[i, :, :c0].astype(BF16),
            wlr=jnp.pad(w_in[i, :, c0:c1], ((0, 0), (0, LANES - 2 * GLA_RANK))).astype(BF16),
            wgk=wgk.astype(BF16),
            bgk=jnp.concatenate([b_gk_f[i], b_gk_b[i]]).reshape(1, -1),
            wn=w_in[i, :, c1:c2].astype(BF16),
            na_bias=_na_bias_table(rpb[i]),
            ng=row(gla_norm_g[i]),
            wog=w_out[i, :GLA_V].astype(BF16), won=w_out[i, GLA_V:].astype(BF16),
            g1=row(ln1_g[i]), b1=row(ln1_b[i]),
            wrh=wr.astype(BF16), wrl=(wr - wr.astype(BF16).astype(F32)).astype(BF16),
            br=jnp.pad(b_router[i].astype(F32), (0, LANES - N_EXPERTS)).reshape(1, -1),
            layer=i, wgu=w_gu, bgu=b_gu[i][:, None, :], wd=w_down, bd=b_down[i][:, None, :],
            wpp=w_ple_proj[i].astype(BF16), wpg=w_ple_gate[i].astype(BF16), bpg=row(b_ple_gate[i]),
            g2=row(ln2_g[i]), b2=row(ln2_b[i]),
        )
        layers.append(lw)
    projected = None
    for i, lw in enumerate(layers):
        last = i == depth - 1
        out_rows = [a.shape[0] for a in x_parts] if last else [n_all]
        outs = _layer(x_parts if i == 0 else [x], projected, ple_parts, lw, None if last else layers[i + 1], groups,
                      out_rows, alpha=alpha, ln0=(row(emb_ln_g), row(emb_ln_b), i == 0))
        if not last:
            x, projected = outs[0], outs[1:]
    return (outs[0].reshape(x_prompt.shape), outs[1].reshape(x_sample.shape))
```
